```python
import jax, jax.numpy as jnp
from jax import lax
import numpy as np

D_MODEL = 2048
BATCH = 2
SEQ = 8192
DEPTH = 1

CHUNK = 64
Q_BLOCK = 128
EPS = 1e-6

CONV_WIDTH = 1024
CONV_K = 3

MLA_HEADS = 16
Q_LORA = 512
KV_LORA = 512
QK_NOPE = 128
QK_ROPE = 64
V_HEAD = 128
MLA_WIDTH = MLA_HEADS * V_HEAD
ROPE_THETA = 10000.0

MEM_TOKENS = 256
MEM_HEADS = 4
MEM_HEAD_DIM = 256
MEM_WIDTH = MEM_HEADS * MEM_HEAD_DIM

IN_SPLITS = (
    CONV_WIDTH, CONV_WIDTH, CONV_WIDTH, CONV_WIDTH,
    Q_LORA, KV_LORA, QK_ROPE, MLA_WIDTH,
    MEM_WIDTH, MEM_WIDTH,
    D_MODEL, D_MODEL, D_MODEL,
)
IN_WIDTH = sum(IN_SPLITS)

kernel_name = "hybrid_conv_mla_memory_block"


def rms_norm(x, g):
    xf = x.astype(jnp.float32)
    y = xf * lax.rsqrt(jnp.mean(xf * xf, axis=-1, keepdims=True) + EPS)
    return (y * g.astype(jnp.float32)).astype(x.dtype)


def apply_rope(x, cos, sin):
    x1, x2 = jnp.split(x.astype(jnp.float32), 2, axis=-1)
    return jnp.concatenate([x1 * cos - x2 * sin, x2 * cos + x1 * sin], axis=-1).astype(x.dtype)


def causal_depthwise_conv(u, w):
    k, c = w.shape
    return lax.conv_general_dilated(
        u, w[:, None, :].astype(u.dtype), window_strides=(1,), padding=[(k - 1, 0)],
        dimension_numbers=("NWC", "WIO", "NWC"), feature_group_count=c)


def mla_attention(c_q, c_kv, k_rope_raw, cos, sin, q_norm_g, w_uq, kv_norm_g, w_ukv,
                  qn_nope_g, qn_rope_g, kn_nope_g, kn_rope_g):
    b, s, _ = c_q.shape
    q = (rms_norm(c_q, q_norm_g) @ w_uq).reshape(b, s, MLA_HEADS, QK_NOPE + QK_ROPE)
    q_nope = rms_norm(q[..., :QK_NOPE], qn_nope_g)
    q_rope = apply_rope(rms_norm(q[..., QK_NOPE:], qn_rope_g), cos[:, :, None], sin[:, :, None])
    kv = (rms_norm(c_kv, kv_norm_g) @ w_ukv).reshape(b, s, MLA_HEADS, QK_NOPE + V_HEAD)
    k_nope = rms_norm(kv[..., :QK_NOPE], kn_nope_g)
    v = kv[..., QK_NOPE:]
    k_rope = apply_rope(rms_norm(k_rope_raw, kn_rope_g), cos, sin)
    scale = (QK_NOPE + QK_ROPE) ** -0.5
    n_blk = s // Q_BLOCK
    qn_blocks = q_nope.reshape(b, n_blk, Q_BLOCK, MLA_HEADS, QK_NOPE).transpose(1, 0, 2, 3, 4)
    qr_blocks = q_rope.reshape(b, n_blk, Q_BLOCK, MLA_HEADS, QK_ROPE).transpose(1, 0, 2, 3, 4)
    k_chunk = jnp.arange(s) // CHUNK

    def attend(args):
        qn, qr, blk = args
        sc = (jnp.einsum("bqhd,bkhd->bhqk", qn, k_nope, preferred_element_type=jnp.float32)
              + jnp.einsum("bqhr,bkr->bhqk", qr, k_rope, preferred_element_type=jnp.float32))
        q_chunk = (blk * Q_BLOCK + jnp.arange(Q_BLOCK)) // CHUNK
        allowed = k_chunk[None, :] <= q_chunk[:, None]
        p = jax.nn.softmax(jnp.where(allowed, sc * scale, -jnp.inf), axis=-1)
        return jnp.einsum("bhqk,bkhd->bqhd", p.astype(v.dtype), v)

    o = lax.map(attend, (qn_blocks, qr_blocks, jnp.arange(n_blk)))
    return o.transpose(1, 0, 2, 3, 4).reshape(b, s, MLA_WIDTH)


def memory_attention(q_raw, mem, mem_norm_g, w_mem_kv, qn_g, kn_g):
    b, s, _ = q_raw.shape
    m = mem.shape[1]
    q = rms_norm(q_raw.reshape(b, s, MEM_HEADS, MEM_HEAD_DIM), qn_g)
    k, v = jnp.split(rms_norm(mem, mem_norm_g) @ w_mem_kv, 2, axis=-1)
    k = rms_norm(k.reshape(b, m, MEM_HEADS, MEM_HEAD_DIM), kn_g)
    v = v.reshape(b, m, MEM_HEADS, MEM_HEAD_DIM)
    sc = jnp.einsum("bqhd,bmhd->bhqm", q, k, preferred_element_type=jnp.float32) * (MEM_HEAD_DIM ** -0.5)
    p = jax.nn.softmax(sc, axis=-1)
    return jnp.einsum("bhqm,bmhd->bqhd", p.astype(v.dtype), v).reshape(b, s, MEM_WIDTH)


def setup_inputs(seed: int = 0) -> dict:
    key = jax.random.key(seed)
    ks = jax.random.split(key, 24)
    f32 = jnp.float32

    def w(k, shape, fan_in):
        return jax.random.normal(k, shape, f32) * (fan_in ** -0.5)

    def gain(k, shape):
        return 1.0 + 0.02 * jax.random.normal(k, shape, f32)

    x = jax.random.normal(ks[0], (BATCH, SEQ, D_MODEL), f32)
    offsets = jax.random.randint(ks[1], (BATCH, 1), 0, 64, dtype=jnp.int32) * CHUNK
    positions = (offsets + jnp.arange(SEQ, dtype=jnp.int32)[None, :]).astype(jnp.int32)
    mem = jax.random.normal(ks[2], (BATCH, MEM_TOKENS, D_MODEL), f32)
    L = DEPTH
    return {
        "x": x,
        "positions": positions,
        "mem": mem,
        "norm_g": gain(ks[3], (L, D_MODEL)),
        "w_in": w(ks[4], (L, D_MODEL, IN_WIDTH), D_MODEL),
        "conv_w": w(ks[5], (L, CONV_K, CONV_WIDTH), CONV_K),
        "w_conv_out": w(ks[6], (L, CONV_WIDTH, D_MODEL), CONV_WIDTH),
        "mla_q_norm_g": gain(ks[7], (L, Q_LORA)),
        "w_uq": w(ks[8], (L, Q_LORA, MLA_HEADS * (QK_NOPE + QK_ROPE)), Q_LORA),
        "mla_kv_norm_g": gain(ks[9], (L, KV_LORA)),
        "w_ukv": w(ks[10], (L, KV_LORA, MLA_HEADS * (QK_NOPE + V_HEAD)), KV_LORA),
        "mla_qn_nope_g": gain(ks[11], (L, QK_NOPE)),
        "mla_qn_rope_g": gain(ks[12], (L, QK_ROPE)),
        "mla_kn_nope_g": gain(ks[13], (L, QK_NOPE)),
        "mla_kn_rope_g": gain(ks[14], (L, QK_ROPE)),
        "w_mla_out": w(ks[15], (L, MLA_WIDTH, D_MODEL), MLA_WIDTH),
        "mem_norm_g": gain(ks[16], (L, D_MODEL)),
        "w_mem_kv": w(ks[17], (L, D_MODEL, 2 * MEM_WIDTH), D_MODEL),
        "mem_qn_g": gain(ks[18], (L, MEM_HEAD_DIM)),
        "mem_kn_g": gain(ks[19], (L, MEM_HEAD_DIM)),
        "w_mem_out": w(ks[20], (L, MEM_WIDTH, D_MODEL), MEM_WIDTH),
        "w_o": w(ks[21], (L, D_MODEL, D_MODEL), D_MODEL),
    }


def reference(x, positions, mem, norm_g, w_in, conv_w, w_conv_out, mla_q_norm_g, w_uq,
              mla_kv_norm_g, w_ukv, mla_qn_nope_g, mla_qn_rope_g, mla_kn_nope_g, mla_kn_rope_g,
              w_mla_out, mem_norm_g, w_mem_kv, mem_qn_g, mem_kn_g, w_mem_out, w_o):
    half = QK_ROPE // 2
    inv_freq = jnp.power(ROPE_THETA, -jnp.arange(half, dtype=jnp.float32) / half)
    ang = positions.astype(jnp.float32)[..., None] * inv_freq
    cos, sin = jnp.cos(ang), jnp.sin(ang)
    split_at = np.cumsum(IN_SPLITS)[:-1].tolist()

    for l in range(DEPTH):
        h = rms_norm(x, norm_g[l])
        proj = h @ w_in[l]
        (c_gate, b_gate, u, conv_z, c_q, c_kv, k_rope_raw, mla_z,
         mem_q, mem_z, g_conv, g_mla, g_mem) = jnp.split(proj, split_at, axis=-1)

        conv_y = b_gate * causal_depthwise_conv(c_gate * u, conv_w[l])
        o_conv = (conv_y * jax.nn.silu(conv_z)) @ w_conv_out[l]

        mla_y = mla_attention(c_q, c_kv, k_rope_raw, cos, sin, mla_q_norm_g[l], w_uq[l],
                              mla_kv_norm_g[l], w_ukv[l], mla_qn_nope_g[l], mla_qn_rope_g[l],
                              mla_kn_nope_g[l], mla_kn_rope_g[l])
        o_mla = (mla_y * jax.nn.silu(mla_z)) @ w_mla_out[l]

        mem_y = memory_attention(mem_q, mem, mem_norm_g[l], w_mem_kv[l], mem_qn_g[l], mem_kn_g[l])
        o_mem = (mem_y * jax.nn.silu(mem_z)) @ w_mem_out[l]

        merged = (jax.nn.sigmoid(g_conv) * o_conv + jax.nn.sigmoid(g_mla) * o_mla
                  + jax.nn.sigmoid(g_mem) * o_mem)
        x = x + merged @ w_o[l]
    return x
```

```python
import functools

import jax
import jax.numpy as jnp
from jax import lax
from jax.experimental import pallas as pl
from jax.experimental.pallas import tpu as pltpu

F32 = jnp.float32
BF16 = jnp.bfloat16

EPS = 1e-6
CHUNK = 64
CONV_WIDTH = 1024
CONV_K = 3
CONV_GROUP = 256
MLA_HEADS = 16
Q_LORA = 512
KV_LORA = 512
QK_NOPE = 128
QK_ROPE = 64
V_HEAD = 128
MLA_WIDTH = MLA_HEADS * V_HEAD
HEAD_PAD = 256
ROPE_THETA = 10000.0
MEM_HEADS = 4
MEM_HEAD_DIM = 256
MEM_WIDTH = MEM_HEADS * MEM_HEAD_DIM
NEG_BIG = -1e30

VMEM_LIMIT_BYTES = 52 * 1024 * 1024


def _params(*sem):
    return pltpu.CompilerParams(dimension_semantics=sem, vmem_limit_bytes=VMEM_LIMIT_BYTES)


def _rms(x, width):
    ms = jnp.sum(x * x, axis=-1, keepdims=True) * (1.0 / width)
    return x * lax.rsqrt(ms + EPS)


def _sigmoid(x):
    return 1.0 / (1.0 + jnp.exp(-x))


def _dot(a, b):
    return jnp.dot(a, b, preferred_element_type=F32)


def _dot_nt(a, b):
    return lax.dot_general(a, b, (((1,), (1,)), ((), ())), preferred_element_type=F32)


def _rmsnorm_kernel(x_ref, g_ref, o_ref):
    x = x_ref[...]
    o_ref[...] = (_rms(x, x.shape[-1]) * g_ref[...]).astype(o_ref.dtype)


def _rmsnorm(x2d, g, tm):
    t, d = x2d.shape
    return pl.pallas_call(
        _rmsnorm_kernel,
        out_shape=jax.ShapeDtypeStruct((t, d), BF16),
        grid=(t // tm,),
        in_specs=[pl.BlockSpec((tm, d), lambda i: (i, 0)),
                  pl.BlockSpec((1, d), lambda i: (0, 0))],
        out_specs=pl.BlockSpec((tm, d), lambda i: (i, 0)),
        compiler_params=_params("arbitrary"),
        name="rmsnorm",
    )(x2d, g)


def _conv_kernel(h_ref, w_ref, cw_ref, o_ref, carry_ref, buf_ref, *, tm, tiles_per_seq):
    i = pl.program_id(0)
    g = pl.program_id(1)
    p = _dot(h_ref[...], w_ref[...])
    c = p[:, 0:CONV_GROUP]
    b = p[:, CONV_GROUP:2 * CONV_GROUP]
    u = p[:, 2 * CONV_GROUP:3 * CONV_GROUP]
    z = p[:, 3 * CONV_GROUP:4 * CONV_GROUP]
    cu = c * u
    seq_start = (i % tiles_per_seq) == 0
    buf_ref[0:8, :] = jnp.where(seq_start, 0.0, carry_ref[g])
    buf_ref[8:8 + tm, :] = cu
    carry_ref[g] = cu[tm - 8:tm, :]
    cw = cw_ref[...]
    y = (buf_ref[6:6 + tm, :] * cw[0:1, :] + buf_ref[7:7 + tm, :] * cw[1:2, :] + cu * cw[2:3, :])
    o_ref[...] = (b * y * (z * _sigmoid(z))).astype(o_ref.dtype)


def _conv_branch(h, w_conv, conv_w, seq, tm):
    t, d = h.shape
    n_groups = CONV_WIDTH // CONV_GROUP
    kern = functools.partial(_conv_kernel, tm=tm, tiles_per_seq=seq // tm)
    return pl.pallas_call(
        kern,
        out_shape=jax.ShapeDtypeStruct((t, CONV_WIDTH), BF16),
        grid=(t // tm, n_groups),
        in_specs=[pl.BlockSpec((tm, d), lambda i, g: (i, 0)),
                  pl.BlockSpec((d, 4 * CONV_GROUP), lambda i, g: (0, g)),
                  pl.BlockSpec((CONV_K, CONV_GROUP), lambda i, g: (0, g))],
        out_specs=pl.BlockSpec((tm, CONV_GROUP), lambda i, g: (i, g)),
        scratch_shapes=[pltpu.VMEM((n_groups, 8, CONV_GROUP), F32),
                        pltpu.VMEM((tm + 8, CONV_GROUP), F32)],
        compiler_params=_params("arbitrary", "arbitrary"),
        name="conv_branch",
    )(h, w_conv, conv_w)


def _mla_prep_kernel(h_ref, win_ref, wuq_ref, wukv_ref, gq_ref, gkv_ref, gqn_ref, gqr_ref,
                     gkn_ref, gkr_ref, cos_ref, sin_ref, q_ref, k_ref, v_ref):
    scale = (QK_NOPE + QK_ROPE) ** -0.5
    p = _dot(h_ref[...], win_ref[...])
    cq = p[:, 0:Q_LORA]
    ckv = p[:, Q_LORA:Q_LORA + KV_LORA]
    kr = p[:, Q_LORA + KV_LORA:]
    q = _dot((_rms(cq, Q_LORA) * gq_ref[...]).astype(BF16), wuq_ref[...])
    kv = _dot((_rms(ckv, KV_LORA) * gkv_ref[...]).astype(BF16), wukv_ref[...])

    cosv = cos_ref[...]
    sinv = sin_ref[...]
    low = lax.broadcasted_iota(jnp.int32, cosv.shape, 1) < QK_ROPE // 2

    def rope(v):
        partner = jnp.where(low, pltpu.roll(v, 128 - QK_ROPE // 2, 1), pltpu.roll(v, QK_ROPE // 2, 1))
        return v * cosv + partner * sinv

    k_rope = rope(_rms(kr, QK_ROPE) * gkr_ref[...]).astype(BF16)
    gqn = gqn_ref[...] * scale
    gqr = gqr_ref[...]
    gkn = gkn_ref[...]
    for hd in range(MLA_HEADS):
        lo = hd * HEAD_PAD
        qn = q[:, lo:lo + QK_NOPE]
        qr = q[:, lo + QK_NOPE:lo + HEAD_PAD]
        q_ref[0, hd, :, 0:QK_NOPE] = (_rms(qn, QK_NOPE) * gqn).astype(BF16)
        q_ref[0, hd, :, QK_NOPE:HEAD_PAD] = (rope(_rms(qr, QK_ROPE) * gqr) * scale).astype(BF16)
        kn = kv[:, lo:lo + QK_NOPE]
        k_ref[0, hd, :, 0:QK_NOPE] = (_rms(kn, QK_NOPE) * gkn).astype(BF16)
        k_ref[0, hd, :, QK_NOPE:HEAD_PAD] = k_rope
        v_ref[0, hd, :, :] = kv[:, lo + QK_NOPE:lo + HEAD_PAD].astype(BF16)


def _mla_prep(h, w_in, w_uq, w_ukv, gq, gkv, gqn, gqr, gkn, gkr, cosv, sinv, batch, seq, tm):
    t, d = h.shape
    ns = seq // tm
    const = lambda b, s: (0, 0)
    row = lambda b, s: (b * ns + s, 0)
    qk_shape = jax.ShapeDtypeStruct((batch, MLA_HEADS, seq, HEAD_PAD), BF16)
    v_shape = jax.ShapeDtypeStruct((batch, MLA_HEADS, seq, V_HEAD), BF16)
    return pl.pallas_call(
        _mla_prep_kernel,
        out_shape=(qk_shape, qk_shape, v_shape),
        grid=(batch, ns),
        in_specs=[pl.BlockSpec((tm, d), row),
                  pl.BlockSpec(w_in.shape, const),
                  pl.BlockSpec(w_uq.shape, const),
                  pl.BlockSpec(w_ukv.shape, const),
                  pl.BlockSpec(gq.shape, const),
                  pl.BlockSpec(gkv.shape, const),
                  pl.BlockSpec(gqn.shape, const),
                  pl.BlockSpec(gqr.shape, const),
                  pl.BlockSpec(gkn.shape, const),
                  pl.BlockSpec(gkr.shape, const),
                  pl.BlockSpec((tm, 128), row),
                  pl.BlockSpec((tm, 128), row)],
        out_specs=(pl.BlockSpec((1, MLA_HEADS, tm, HEAD_PAD), lambda b, s: (b, 0, s, 0)),
                   pl.BlockSpec((1, MLA_HEADS, tm, HEAD_PAD), lambda b, s: (b, 0, s, 0)),
                   pl.BlockSpec((1, MLA_HEADS, tm, V_HEAD), lambda b, s: (b, 0, s, 0))),
        compiler_params=_params("arbitrary", "arbitrary"),
        name="mla_prep",
    )(h, w_in, w_uq, w_ukv, gq, gkv, gqn, gqr, gkn, gkr, cosv, sinv)


def _attn_kernel(q_ref, k_ref, v_ref, sz_ref, o_ref, *, tq):
    qi = pl.program_id(2)
    q = q_ref[0, 0]

    def step(ki, carry, diagonal):
        m, l, acc = carry
        start = pl.multiple_of(ki * tq, tq)
        k = k_ref[0, 0, pl.ds(start, tq), :]
        v = v_ref[0, 0, pl.ds(start, tq), :]
        s = _dot_nt(q, k)
        if diagonal:
            r = lax.broadcasted_iota(jnp.int32, s.shape, 0) // CHUNK
            c = lax.broadcasted_iota(jnp.int32, s.shape, 1) // CHUNK
            s = jnp.where(c <= r, s, NEG_BIG)
        m_new = jnp.maximum(m, jnp.max(s, axis=-1, keepdims=True))
        alpha = jnp.exp(m - m_new)
        p = jnp.exp(s - m_new)
        l = alpha * l + jnp.sum(p, axis=-1, keepdims=True)
        acc = alpha * acc + _dot(p.astype(BF16), v)
        return m_new, l, acc

    init = (jnp.full((tq, 1), NEG_BIG, F32), jnp.zeros((tq, 1), F32), jnp.zeros((tq, V_HEAD), F32))
    carry = lax.fori_loop(0, qi, lambda ki, cr: step(ki, cr, False), init)
    _, l, acc = step(qi, carry, True)
    o_ref[0] = (acc * (1.0 / l) * sz_ref[0].astype(F32)).astype(o_ref.dtype)


def _mla_attention(q, k, v, sz, tq):
    batch, heads, seq, _ = q.shape
    kern = functools.partial(_attn_kernel, tq=tq)
    return pl.pallas_call(
        kern,
        out_shape=jax.ShapeDtypeStruct((batch, seq, MLA_WIDTH), BF16),
        grid=(batch, heads, seq // tq),
        in_specs=[pl.BlockSpec((1, 1, tq, HEAD_PAD), lambda b, h, i: (b, h, i, 0)),
                  pl.BlockSpec((1, 1, seq, HEAD_PAD), lambda b, h, i: (b, h, 0, 0)),
                  pl.BlockSpec((1, 1, seq, V_HEAD), lambda b, h, i: (b, h, 0, 0)),
                  pl.BlockSpec((1, tq, V_HEAD), lambda b, h, i: (b, i, h))],
        out_specs=pl.BlockSpec((1, tq, V_HEAD), lambda b, h, i: (b, i, h)),
        compiler_params=_params("arbitrary", "arbitrary", "arbitrary"),
        name="mla_attention",
    )(q, k, v, sz)


def _mem_kv_kernel(mem_ref, g_ref, w_ref, gk_ref, k_ref, v_ref):
    m = mem_ref[0]
    kv = _dot((_rms(m, m.shape[-1]) * g_ref[...]).astype(BF16), w_ref[...])
    gk = gk_ref[...]
    for hd in range(MEM_HEADS):
        lo = hd * MEM_HEAD_DIM
        k_ref[0, :, lo:lo + MEM_HEAD_DIM] = (_rms(kv[:, lo:lo + MEM_HEAD_DIM], MEM_HEAD_DIM) * gk).astype(BF16)
    v_ref[0] = kv[:, MEM_WIDTH:].astype(BF16)


def _mem_kv(mem, g, w, gk):
    batch, m, d = mem.shape
    const = lambda b: (0, 0)
    shape = jax.ShapeDtypeStruct((batch, m, MEM_WIDTH), BF16)
    return pl.pallas_call(
        _mem_kv_kernel,
        out_shape=(shape, shape),
        grid=(batch,),
        in_specs=[pl.BlockSpec((1, m, d), lambda b: (b, 0, 0)),
                  pl.BlockSpec(g.shape, const),
                  pl.BlockSpec(w.shape, const),
                  pl.BlockSpec(gk.shape, const)],
        out_specs=(pl.BlockSpec((1, m, MEM_WIDTH), lambda b: (b, 0, 0)),
                   pl.BlockSpec((1, m, MEM_WIDTH), lambda b: (b, 0, 0))),
        compiler_params=_params("arbitrary"),
        name="mem_kv",
    )(mem, g, w, gk)


def _mem_attn_kernel(h_ref, w_ref, k_ref, v_ref, gq_ref, o_ref):
    p = _dot(h_ref[...], w_ref[...])
    gq = gq_ref[...] * (MEM_HEAD_DIM ** -0.5)
    for hd in range(MEM_HEADS):
        lo = hd * MEM_HEAD_DIM
        q = (_rms(p[:, lo:lo + MEM_HEAD_DIM], MEM_HEAD_DIM) * gq).astype(BF16)
        z = p[:, MEM_WIDTH + lo:MEM_WIDTH + lo + MEM_HEAD_DIM]
        s = _dot_nt(q, k_ref[0, :, lo:lo + MEM_HEAD_DIM])
        e = jnp.exp(s - jnp.max(s, axis=-1, keepdims=True))
        y = _dot(e.astype(BF16), v_ref[0, :, lo:lo + MEM_HEAD_DIM])
        y = y * (1.0 / jnp.sum(e, axis=-1, keepdims=True))
        o_ref[:, lo:lo + MEM_HEAD_DIM] = (y * (z * _sigmoid(z))).astype(o_ref.dtype)


def _mem_attention(h, w, k_mem, v_mem, gq, batch, seq, tm):
    t, d = h.shape
    ns = seq // tm
    m = k_mem.shape[1]
    const = lambda b, s: (0, 0)
    return pl.pallas_call(
        _mem_attn_kernel,
        out_shape=jax.ShapeDtypeStruct((t, MEM_WIDTH), BF16),
        grid=(batch, ns),
        in_specs=[pl.BlockSpec((tm, d), lambda b, s: (b * ns + s, 0)),
                  pl.BlockSpec(w.shape, const),
                  pl.BlockSpec((1, m, MEM_WIDTH), lambda b, s: (b, 0, 0)),
                  pl.BlockSpec((1, m, MEM_WIDTH), lambda b, s: (b, 0, 0)),
                  pl.BlockSpec(gq.shape, const)],
        out_specs=pl.BlockSpec((tm, MEM_WIDTH), lambda b, s: (b * ns + s, 0)),
        compiler_params=_params("arbitrary", "arbitrary"),
        name="mem_attention",
    )(h, w, k_mem, v_mem, gq)


def _gate_kernel(h_ref, w_ref, o_ref, *, silu):
    p = _dot(h_ref[...], w_ref[...])
    s = _sigmoid(p)
    o_ref[...] = ((p * s) if silu else s).astype(o_ref.dtype)


def _gate_proj(h, w, silu, tm, tn):
    t, d = h.shape
    n = w.shape[1]
    return pl.pallas_call(
        functools.partial(_gate_kernel, silu=silu),
        out_shape=jax.ShapeDtypeStruct((t, n), BF16),
        grid=(t // tm, n // tn),
        in_specs=[pl.BlockSpec((tm, d), lambda i, j: (i, 0)),
                  pl.BlockSpec((d, tn), lambda i, j: (0, j))],
        out_specs=pl.BlockSpec((tm, tn), lambda i, j: (i, j)),
        compiler_params=_params("arbitrary", "arbitrary"),
        name="silu_gate_proj" if silu else "sigmoid_gate_proj",
    )(h, w)


def _merge_kernel(ac_ref, am_ref, ae_ref, wc_ref, wm_ref, we_ref, gc_ref, gm_ref, ge_ref, o_ref):
    merged = (gc_ref[...].astype(F32) * _dot(ac_ref[...], wc_ref[...])
              + gm_ref[...].astype(F32) * _dot(am_ref[...], wm_ref[...])
              + ge_ref[...].astype(F32) * _dot(ae_ref[...], we_ref[...]))
    o_ref[...] = merged.astype(o_ref.dtype)


def _merge(a_conv, a_mla, a_mem, w_conv, w_mla, w_mem, gates, tm, tn):
    t = a_conv.shape[0]
    n = w_conv.shape[1]
    nj = n // tn
    act = lambda a: pl.BlockSpec((tm, a.shape[1]), lambda i, j: (i, 0))
    wgt = lambda w: pl.BlockSpec((w.shape[0], tn), lambda i, j: (0, j))
    gate = lambda k: pl.BlockSpec((tm, tn), lambda i, j: (i, k * nj + j))
    return pl.pallas_call(
        _merge_kernel,
        out_shape=jax.ShapeDtypeStruct((t, n), BF16),
        grid=(t // tm, nj),
        in_specs=[act(a_conv), act(a_mla), act(a_mem), wgt(w_conv), wgt(w_mla), wgt(w_mem),
                  gate(0), gate(1), gate(2)],
        out_specs=pl.BlockSpec((tm, tn), lambda i, j: (i, j)),
        compiler_params=_params("arbitrary", "arbitrary"),
        name="merge",
    )(a_conv, a_mla, a_mem, w_conv, w_mla, w_mem, gates, gates, gates)


def _out_kernel(a_ref, w_ref, x_ref, o_ref):
    o_ref[...] = x_ref[...] + _dot(a_ref[...], w_ref[...])


def _out_proj(merged, w_o, x2d, tm, tn):
    t, d = merged.shape
    n = w_o.shape[1]
    return pl.pallas_call(
        _out_kernel,
        out_shape=jax.ShapeDtypeStruct((t, n), F32),
        grid=(t // tm, n // tn),
        in_specs=[pl.BlockSpec((tm, d), lambda i, j: (i, 0)),
                  pl.BlockSpec((d, tn), lambda i, j: (0, j)),
                  pl.BlockSpec((tm, tn), lambda i, j: (i, j))],
        out_specs=pl.BlockSpec((tm, tn), lambda i, j: (i, j)),
        compiler_params=_params("arbitrary", "arbitrary"),
        name="out_proj",
    )(merged, w_o, x2d)


def _pad_lanes(g, width):
    return jnp.pad(g, (0, width - g.shape[0]))[None, :]


def _layer(x, cosv, sinv, mem, norm_g, w_in, conv_w, w_conv_out, mla_q_norm_g, w_uq, mla_kv_norm_g,
           w_ukv, mla_qn_nope_g, mla_qn_rope_g, mla_kn_nope_g, mla_kn_rope_g, w_mla_out, mem_norm_g,
           w_mem_kv, mem_qn_g, mem_kn_g, w_mem_out, w_o):
    batch, seq, d = x.shape
    t = batch * seq
    x2d = x.reshape(t, d)

    o_conv, o_cq = 0, 4 * CONV_WIDTH
    o_mlaz = o_cq + Q_LORA + KV_LORA + QK_ROPE
    o_memq = o_mlaz + MLA_WIDTH
    o_gate = o_memq + 2 * MEM_WIDTH
    wb = w_in.astype(BF16)
    n_groups = CONV_WIDTH // CONV_GROUP
    w_conv_in = (wb[:, o_conv:o_cq].reshape(d, 4, n_groups, CONV_GROUP).transpose(0, 2, 1, 3)
                 .reshape(d, 4 * CONV_WIDTH))
    w_mla_in = jnp.pad(wb[:, o_cq:o_mlaz], ((0, 0), (0, 128 - QK_ROPE)))
    w_mlaz = wb[:, o_mlaz:o_memq]
    w_mem_in = wb[:, o_memq:o_gate]
    w_gates = wb[:, o_gate:]
    w_uq_p = jnp.pad(w_uq.astype(BF16).reshape(Q_LORA, MLA_HEADS, QK_NOPE + QK_ROPE),
                     ((0, 0), (0, 0), (0, HEAD_PAD - QK_NOPE - QK_ROPE))).reshape(Q_LORA, MLA_HEADS * HEAD_PAD)

    h = _rmsnorm(x2d, norm_g[None, :], tm=512)
    conv_act = _conv_branch(h, w_conv_in, conv_w, seq, tm=1024)
    q, k, v = _mla_prep(h, w_mla_in, w_uq_p, w_ukv.astype(BF16), mla_q_norm_g[None, :],
                        mla_kv_norm_g[None, :], mla_qn_nope_g[None, :], _pad_lanes(mla_qn_rope_g, 128),
                        mla_kn_nope_g[None, :], _pad_lanes(mla_kn_rope_g, 128), cosv, sinv,
                        batch, seq, tm=256)
    sz_mla = _gate_proj(h, w_mlaz, True, tm=1024, tn=1024)
    gates = _gate_proj(h, w_gates, False, tm=1024, tn=1024)
    mla_act = _mla_attention(q, k, v, sz_mla.reshape(batch, seq, MLA_WIDTH), tq=512)
    k_mem, v_mem = _mem_kv(mem, mem_norm_g[None, :], w_mem_kv.astype(BF16), mem_kn_g[None, :])
    mem_act = _mem_attention(h, w_mem_in, k_mem, v_mem, mem_qn_g[None, :], batch, seq, tm=512)
    merged = _merge(conv_act, mla_act.reshape(t, MLA_WIDTH), mem_act, w_conv_out.astype(BF16),
                    w_mla_out.astype(BF16), w_mem_out.astype(BF16), gates, tm=1024, tn=512)
    out = _out_proj(merged, w_o.astype(BF16), x2d, tm=1024, tn=1024)
    return out.reshape(batch, seq, d)


def kernel(x, positions, mem, norm_g, w_in, conv_w, w_conv_out, mla_q_norm_g, w_uq, mla_kv_norm_g, w_ukv, mla_qn_nope_g, mla_qn_rope_g, mla_kn_nope_g, mla_kn_rope_g, w_mla_out, mem_norm_g, w_mem_kv, mem_qn_g, mem_kn_g, w_mem_out, w_o):
    batch, seq, _ = x.shape
    half = QK_ROPE // 2
    inv_freq = jnp.power(ROPE_THETA, -jnp.arange(half, dtype=F32) / half)
    ang = positions.astype(F32)[..., None] * inv_freq
    cos, sin = jnp.cos(ang), jnp.sin(ang)
    zeros = jnp.zeros((batch, seq, 128 - QK_ROPE), F32)
    cosv = jnp.concatenate([cos, cos, zeros], axis=-1).reshape(batch * seq, 128)
    sinv = jnp.concatenate([-sin, sin, zeros], axis=-1).reshape(batch * seq, 128)
    for l in range(norm_g.shape[0]):
        x = _layer(x, cosv, sinv, mem, norm_g[l], w_in[l], conv_w[l], w_conv_out[l], mla_q_norm_g[l],
                   w_uq[l], mla_kv_norm_g[l], w_ukv[l], mla_qn_nope_g[l], mla_qn_rope_g[l],
                   mla_kn_nope_g[l], mla_kn_rope_g[l], w_mla_out[l], mem_norm_g[l], w_mem_kv[l],
                   mem_qn_g[l], mem_kn_g[l], w_mem_out[l], w_o[l])
    return x
```

```python
import functools

import jax
import jax.numpy as jnp
from jax import lax
from jax.experimental import pallas as pl
from jax.experimental.pallas import tpu as pltpu

F32 = jnp.float32
BF16 = jnp.bfloat16

EPS = 1e-6
CHUNK = 64
CONV_WIDTH = 1024
CONV_K = 3
CONV_GROUP = 256
MLA_HEADS = 16
Q_LORA = 512
KV_LORA = 512
QK_NOPE = 128
QK_ROPE = 64
V_HEAD = 128
MLA_WIDTH = MLA_HEADS * V_HEAD
HEAD_PAD = 256
ROPE_THETA = 10000.0
MEM_HEADS = 4
MEM_HEAD_DIM = 256
MEM_WIDTH = MEM_HEADS * MEM_HEAD_DIM
NEG_BIG = -1e30
LOG2_E = 1.4426950408889634
ATTN_TILE = 512

VMEM_LIMIT_BYTES = 52 * 1024 * 1024


def _params(*sem):
    return pltpu.CompilerParams(dimension_semantics=sem, vmem_limit_bytes=VMEM_LIMIT_BYTES)


def _rms(x, width):
    ms = jnp.sum(x * x, axis=-1, keepdims=True) * (1.0 / width)
    return x * lax.rsqrt(ms + EPS)


def _sigmoid(x):
    return 1.0 / (1.0 + jnp.exp(-x))


def _dot(a, b):
    return jnp.dot(a, b, preferred_element_type=F32)


def _dot_nt(a, b):
    return lax.dot_general(a, b, (((1,), (1,)), ((), ())), preferred_element_type=F32)


def _rmsnorm_kernel(x_ref, g_ref, o_ref):
    x = x_ref[...]
    o_ref[...] = (_rms(x, x.shape[-1]) * g_ref[...]).astype(o_ref.dtype)


def _rmsnorm(x2d, g, tm):
    t, d = x2d.shape
    return pl.pallas_call(
        _rmsnorm_kernel,
        out_shape=jax.ShapeDtypeStruct((t, d), BF16),
        grid=(t // tm,),
        in_specs=[pl.BlockSpec((tm, d), lambda i: (i, 0)),
                  pl.BlockSpec((1, d), lambda i: (0, 0))],
        out_specs=pl.BlockSpec((tm, d), lambda i: (i, 0)),
        compiler_params=_params("arbitrary"),
        name="rmsnorm",
    )(x2d, g)


def _conv_kernel(h_ref, w_ref, cw_ref, o_ref, carry_ref, buf_ref, *, tm, tiles_per_seq):
    i = pl.program_id(0)
    g = pl.program_id(1)
    p = _dot(h_ref[...], w_ref[...])
    c = p[:, 0:CONV_GROUP]
    b = p[:, CONV_GROUP:2 * CONV_GROUP]
    u = p[:, 2 * CONV_GROUP:3 * CONV_GROUP]
    z = p[:, 3 * CONV_GROUP:4 * CONV_GROUP]
    cu = c * u
    seq_start = (i % tiles_per_seq) == 0
    buf_ref[0:8, :] = jnp.where(seq_start, 0.0, carry_ref[g])
    buf_ref[8:8 + tm, :] = cu
    carry_ref[g] = cu[tm - 8:tm, :]
    cw = cw_ref[...]
    y = (buf_ref[6:6 + tm, :] * cw[0:1, :] + buf_ref[7:7 + tm, :] * cw[1:2, :] + cu * cw[2:3, :])
    o_ref[...] = (b * y * (z * _sigmoid(z))).astype(o_ref.dtype)


def _conv_branch(h, w_conv, conv_w, seq, tm):
    t, d = h.shape
    n_groups = CONV_WIDTH // CONV_GROUP
    kern = functools.partial(_conv_kernel, tm=tm, tiles_per_seq=seq // tm)
    return pl.pallas_call(
        kern,
        out_shape=jax.ShapeDtypeStruct((t, CONV_WIDTH), BF16),
        grid=(t // tm, n_groups),
        in_specs=[pl.BlockSpec((tm, d), lambda i, g: (i, 0)),
                  pl.BlockSpec((d, 4 * CONV_GROUP), lambda i, g: (0, g)),
                  pl.BlockSpec((CONV_K, CONV_GROUP), lambda i, g: (0, g))],
        out_specs=pl.BlockSpec((tm, CONV_GROUP), lambda i, g: (i, g)),
        scratch_shapes=[pltpu.VMEM((n_groups, 8, CONV_GROUP), F32),
                        pltpu.VMEM((tm + 8, CONV_GROUP), F32)],
        compiler_params=_params("arbitrary", "arbitrary"),
        name="conv_branch",
    )(h, w_conv, conv_w)


def _mla_prep_kernel(h_ref, win_ref, wuqt_ref, wkn_ref, wvt_ref, gq_ref, gkv_ref, gqn_ref, gqr_ref,
                     gkn_ref, gkr_ref, cos_ref, sin_ref, cost_ref, sint_ref, qt_ref, k_ref, vt_ref):
    tm = h_ref.shape[0]
    half = QK_ROPE // 2
    p = _dot(h_ref[...], win_ref[...])
    cq = p[:, 0:Q_LORA]
    ckv = p[:, Q_LORA:Q_LORA + KV_LORA]
    kr = p[:, Q_LORA + KV_LORA:]
    cqn = (_rms(cq, Q_LORA) * gq_ref[...]).astype(BF16)
    ckvn = (_rms(ckv, KV_LORA) * gkv_ref[...]).astype(BF16)
    qt = _dot_nt(wuqt_ref[...], cqn)
    kn = _dot(ckvn, wkn_ref[...])
    vt = _dot_nt(wvt_ref[...], ckvn)

    cosv = cos_ref[...]
    sinv = sin_ref[...]
    low = lax.broadcasted_iota(jnp.int32, cosv.shape, 1) < half
    krn = _rms(kr, QK_ROPE) * gkr_ref[...]
    partner = jnp.where(low, pltpu.roll(krn, 128 - half, 1), pltpu.roll(krn, half, 1))
    k_rope = (krn * cosv + partner * sinv).astype(BF16)

    gqn = jnp.broadcast_to(gqn_ref[...], (QK_NOPE, tm))
    gqr = jnp.broadcast_to(gqr_ref[...], (QK_ROPE, tm))
    gkn = gkn_ref[...]
    cost = cost_ref[...]
    sint = sint_ref[...]
    zeros = jnp.zeros((HEAD_PAD - QK_NOPE - QK_ROPE, tm), BF16)
    for hd in range(MLA_HEADS):
        lo = hd * (QK_NOPE + QK_ROPE)
        qn = qt[lo:lo + QK_NOPE, :]
        qr = qt[lo + QK_NOPE:lo + QK_NOPE + QK_ROPE, :]
        qn = qn * lax.rsqrt(jnp.sum(qn * qn, axis=0, keepdims=True) * (1.0 / QK_NOPE) + EPS) * gqn
        qr = qr * lax.rsqrt(jnp.sum(qr * qr, axis=0, keepdims=True) * (1.0 / QK_ROPE) + EPS) * gqr
        x1 = qr[0:half, :]
        x2 = qr[half:QK_ROPE, :]
        qt_ref[0, hd, 0:QK_NOPE, :] = qn.astype(BF16)
        qt_ref[0, hd, QK_NOPE:QK_NOPE + half, :] = (x1 * cost - x2 * sint).astype(BF16)
        qt_ref[0, hd, QK_NOPE + half:QK_NOPE + QK_ROPE, :] = (x2 * cost + x1 * sint).astype(BF16)
        qt_ref[0, hd, QK_NOPE + QK_ROPE:HEAD_PAD, :] = zeros
        kh = kn[:, hd * QK_NOPE:(hd + 1) * QK_NOPE]
        k_ref[0, hd, :, 0:QK_NOPE] = (_rms(kh, QK_NOPE) * gkn).astype(BF16)
        k_ref[0, hd, :, QK_NOPE:HEAD_PAD] = k_rope
        vt_ref[0, hd, 0, :, :] = vt[hd * V_HEAD:(hd + 1) * V_HEAD, :].astype(BF16)


def _mla_prep(h, w_in, w_uqt, w_kn, w_vt, gq, gkv, gqn, gqr, gkn, gkr, cosv, sinv, cost, sint,
              batch, seq, tm, tk):
    t, d = h.shape
    ns = seq // tm
    per_chunk = tk // tm
    const = lambda b, s: (0, 0)
    row = lambda b, s: (b * ns + s, 0)
    col = lambda b, s: (0, b * ns + s)
    qt_shape = jax.ShapeDtypeStruct((batch, MLA_HEADS, HEAD_PAD, seq), BF16)
    k_shape = jax.ShapeDtypeStruct((batch, MLA_HEADS, seq, HEAD_PAD), BF16)
    vt_shape = jax.ShapeDtypeStruct((batch, MLA_HEADS, seq // tk, V_HEAD, tk), BF16)
    full = lambda a: pl.BlockSpec(a.shape, const)
    return pl.pallas_call(
        _mla_prep_kernel,
        out_shape=(qt_shape, k_shape, vt_shape),
        grid=(batch, ns),
        in_specs=[pl.BlockSpec((tm, d), row), full(w_in), full(w_uqt), full(w_kn), full(w_vt),
                  full(gq), full(gkv), full(gqn), full(gqr), full(gkn), full(gkr),
                  pl.BlockSpec((tm, 128), row), pl.BlockSpec((tm, 128), row),
                  pl.BlockSpec((QK_ROPE // 2, tm), col), pl.BlockSpec((QK_ROPE // 2, tm), col)],
        out_specs=(pl.BlockSpec((1, MLA_HEADS, HEAD_PAD, tm), lambda b, s: (b, 0, 0, s)),
                   pl.BlockSpec((1, MLA_HEADS, tm, HEAD_PAD), lambda b, s: (b, 0, s, 0)),
                   pl.BlockSpec((1, MLA_HEADS, 1, V_HEAD, tm),
                                lambda b, s: (b, 0, s // per_chunk, 0, s % per_chunk))),
        compiler_params=_params("arbitrary", "arbitrary"),
        name="mla_prep",
    )(h, w_in, w_uqt, w_kn, w_vt, gq, gkv, gqn, gqr, gkn, gkr, cosv, sinv, cost, sint)


def _attn_kernel(qt_ref, k_ref, vt_ref, sz_ref, o_ref, acc_ref, m_ref, l_ref, s0_ref, s1_ref,
                 *, tq, heads):
    qi = pl.program_id(2)
    s_bufs = (s0_ref, s1_ref)

    def scores(hh, ki, dst):
        start = pl.multiple_of(ki * tq, tq)
        dst[hh] = _dot(k_ref[0, hh, pl.ds(start, tq), :], qt_ref[0, hh])

    def update(hh, ki, src, diagonal):
        s = src[hh]
        if diagonal:
            kc = lax.broadcasted_iota(jnp.int32, s.shape, 0) // CHUNK
            qc = lax.broadcasted_iota(jnp.int32, s.shape, 1) // CHUNK
            s = jnp.where(kc <= qc, s, NEG_BIG)
        m = m_ref[hh]
        m_new = jnp.maximum(m, jnp.max(s, axis=0, keepdims=True))
        alpha = jnp.exp2(m - m_new)
        p = jnp.exp2(s - m_new)
        m_ref[hh] = m_new
        l_ref[hh] = alpha * l_ref[hh] + jnp.sum(p, axis=0, keepdims=True)
        acc_ref[hh] = alpha * acc_ref[hh] + _dot(vt_ref[0, hh, ki], p.astype(BF16))

    acc_ref[...] = jnp.zeros_like(acc_ref)
    l_ref[...] = jnp.zeros_like(l_ref)
    m_ref[...] = jnp.full(m_ref.shape, NEG_BIG, F32)
    for hh in range(heads):
        scores(hh, 0, s0_ref)

    def body(ki, carry):
        for parity in range(2):
            @pl.when(ki % 2 == parity)
            def _():
                for hh in range(heads):
                    scores(hh, ki + 1, s_bufs[1 - parity])
                    update(hh, ki, s_bufs[parity], False)
        return carry

    lax.fori_loop(0, qi, body, 0)
    for parity in range(2):
        @pl.when(qi % 2 == parity)
        def _():
            for hh in range(heads):
                update(hh, qi, s_bufs[parity], True)
                y = (acc_ref[hh] * (1.0 / l_ref[hh])).T
                sz = sz_ref[0, :, hh * V_HEAD:(hh + 1) * V_HEAD].astype(F32)
                o_ref[0, :, hh * V_HEAD:(hh + 1) * V_HEAD] = (y * sz).astype(o_ref.dtype)


def _mla_attention(qt, k, vt, sz, tq, heads):
    batch, n_heads, seq, _ = k.shape
    kern = functools.partial(_attn_kernel, tq=tq, heads=heads)
    return pl.pallas_call(
        kern,
        out_shape=jax.ShapeDtypeStruct((batch, seq, MLA_WIDTH), BF16),
        grid=(batch, n_heads // heads, seq // tq),
        in_specs=[pl.BlockSpec((1, heads, HEAD_PAD, tq), lambda b, h, i: (b, h, 0, i)),
                  pl.BlockSpec((1, heads, seq, HEAD_PAD), lambda b, h, i: (b, h, 0, 0)),
                  pl.BlockSpec((1, heads, seq // tq, V_HEAD, tq), lambda b, h, i: (b, h, 0, 0, 0)),
                  pl.BlockSpec((1, tq, heads * V_HEAD), lambda b, h, i: (b, i, h))],
        out_specs=pl.BlockSpec((1, tq, heads * V_HEAD), lambda b, h, i: (b, i, h)),
        scratch_shapes=[pltpu.VMEM((heads, V_HEAD, tq), F32),
                        pltpu.VMEM((heads, 1, tq), F32),
                        pltpu.VMEM((heads, 1, tq), F32),
                        pltpu.VMEM((heads, tq, tq), F32),
                        pltpu.VMEM((heads, tq, tq), F32)],
        compiler_params=_params("arbitrary", "arbitrary", "arbitrary"),
        name="mla_attention",
    )(qt, k, vt, sz)


def _mem_kv_kernel(mem_ref, g_ref, w_ref, gk_ref, k_ref, v_ref):
    m = mem_ref[0]
    kv = _dot((_rms(m, m.shape[-1]) * g_ref[...]).astype(BF16), w_ref[...])
    gk = gk_ref[...]
    for hd in range(MEM_HEADS):
        lo = hd * MEM_HEAD_DIM
        k_ref[0, :, lo:lo + MEM_HEAD_DIM] = (_rms(kv[:, lo:lo + MEM_HEAD_DIM], MEM_HEAD_DIM) * gk).astype(BF16)
    v_ref[0] = kv[:, MEM_WIDTH:].astype(BF16)


def _mem_kv(mem, g, w, gk):
    batch, m, d = mem.shape
    const = lambda b: (0, 0)
    shape = jax.ShapeDtypeStruct((batch, m, MEM_WIDTH), BF16)
    return pl.pallas_call(
        _mem_kv_kernel,
        out_shape=(shape, shape),
        grid=(batch,),
        in_specs=[pl.BlockSpec((1, m, d), lambda b: (b, 0, 0)),
                  pl.BlockSpec(g.shape, const),
                  pl.BlockSpec(w.shape, const),
                  pl.BlockSpec(gk.shape, const)],
        out_specs=(pl.BlockSpec((1, m, MEM_WIDTH), lambda b: (b, 0, 0)),
                   pl.BlockSpec((1, m, MEM_WIDTH), lambda b: (b, 0, 0))),
        compiler_params=_params("arbitrary"),
        name="mem_kv",
    )(mem, g, w, gk)


def _mem_attn_kernel(h_ref, w_ref, k_ref, v_ref, gq_ref, o_ref):
    p = _dot(h_ref[...], w_ref[...])
    gq = gq_ref[...] * (MEM_HEAD_DIM ** -0.5)
    for hd in range(MEM_HEADS):
        lo = hd * MEM_HEAD_DIM
        q = (_rms(p[:, lo:lo + MEM_HEAD_DIM], MEM_HEAD_DIM) * gq).astype(BF16)
        z = p[:, MEM_WIDTH + lo:MEM_WIDTH + lo + MEM_HEAD_DIM]
        s = _dot_nt(q, k_ref[0, :, lo:lo + MEM_HEAD_DIM])
        e = jnp.exp(s - jnp.max(s, axis=-1, keepdims=True))
        y = _dot(e.astype(BF16), v_ref[0, :, lo:lo + MEM_HEAD_DIM])
        y = y * (1.0 / jnp.sum(e, axis=-1, keepdims=True))
        o_ref[:, lo:lo + MEM_HEAD_DIM] = (y * (z * _sigmoid(z))).astype(o_ref.dtype)


def _mem_attention(h, w, k_mem, v_mem, gq, batch, seq, tm):
    t, d = h.shape
    ns = seq // tm
    m = k_mem.shape[1]
    const = lambda b, s: (0, 0)
    return pl.pallas_call(
        _mem_attn_kernel,
        out_shape=jax.ShapeDtypeStruct((t, MEM_WIDTH), BF16),
        grid=(batch, ns),
        in_specs=[pl.BlockSpec((tm, d), lambda b, s: (b * ns + s, 0)),
                  pl.BlockSpec(w.shape, const),
                  pl.BlockSpec((1, m, MEM_WIDTH), lambda b, s: (b, 0, 0)),
                  pl.BlockSpec((1, m, MEM_WIDTH), lambda b, s: (b, 0, 0)),
                  pl.BlockSpec(gq.shape, const)],
        out_specs=pl.BlockSpec((tm, MEM_WIDTH), lambda b, s: (b * ns + s, 0)),
        compiler_params=_params("arbitrary", "arbitrary"),
        name="mem_attention",
    )(h, w, k_mem, v_mem, gq)


def _gate_kernel(h_ref, w_ref, o_ref, *, silu):
    p = _dot(h_ref[...], w_ref[...])
    s = _sigmoid(p)
    o_ref[...] = ((p * s) if silu else s).astype(o_ref.dtype)


def _gate_proj(h, w, silu, tm, tn):
    t, d = h.shape
    n = w.shape[1]
    return pl.pallas_call(
        functools.partial(_gate_kernel, silu=silu),
        out_shape=jax.ShapeDtypeStruct((t, n), BF16),
        grid=(t // tm, n // tn),
        in_specs=[pl.BlockSpec((tm, d), lambda i, j: (i, 0)),
                  pl.BlockSpec((d, tn), lambda i, j: (0, j))],
        out_specs=pl.BlockSpec((tm, tn), lambda i, j: (i, j)),
        compiler_params=_params("arbitrary", "arbitrary"),
        name="silu_gate_proj" if silu else "sigmoid_gate_proj",
    )(h, w)


def _merge_kernel(ac_ref, am_ref, ae_ref, wc_ref, wm_ref, we_ref, gc_ref, gm_ref, ge_ref, o_ref):
    merged = (gc_ref[...].astype(F32) * _dot(ac_ref[...], wc_ref[...])
              + gm_ref[...].astype(F32) * _dot(am_ref[...], wm_ref[...])
              + ge_ref[...].astype(F32) * _dot(ae_ref[...], we_ref[...]))
    o_ref[...] = merged.astype(o_ref.dtype)


def _merge(a_conv, a_mla, a_mem, w_conv, w_mla, w_mem, gates, tm, tn):
    t = a_conv.shape[0]
    n = w_conv.shape[1]
    nj = n // tn
    act = lambda a: pl.BlockSpec((tm, a.shape[1]), lambda i, j: (i, 0))
    wgt = lambda w: pl.BlockSpec((w.shape[0], tn), lambda i, j: (0, j))
    gate = lambda k: pl.BlockSpec((tm, tn), lambda i, j: (i, k * nj + j))
    return pl.pallas_call(
        _merge_kernel,
        out_shape=jax.ShapeDtypeStruct((t, n), BF16),
        grid=(t // tm, nj),
        in_specs=[act(a_conv), act(a_mla), act(a_mem), wgt(w_conv), wgt(w_mla), wgt(w_mem),
                  gate(0), gate(1), gate(2)],
        out_specs=pl.BlockSpec((tm, tn), lambda i, j: (i, j)),
        compiler_params=_params("arbitrary", "arbitrary"),
        name="merge",
    )(a_conv, a_mla, a_mem, w_conv, w_mla, w_mem, gates, gates, gates)


def _out_kernel(a_ref, w_ref, x_ref, o_ref):
    o_ref[...] = x_ref[...] + _dot(a_ref[...], w_ref[...])


def _out_proj(merged, w_o, x2d, tm, tn):
    t, d = merged.shape
    n = w_o.shape[1]
    return pl.pallas_call(
        _out_kernel,
        out_shape=jax.ShapeDtypeStruct((t, n), F32),
        grid=(t // tm, n // tn),
        in_specs=[pl.BlockSpec((tm, d), lambda i, j: (i, 0)),
                  pl.BlockSpec((d, tn), lambda i, j: (0, j)),
                  pl.BlockSpec((tm, tn), lambda i, j: (i, j))],
        out_specs=pl.BlockSpec((tm, tn), lambda i, j: (i, j)),
        compiler_params=_params("arbitrary", "arbitrary"),
        name="out_proj",
    )(merged, w_o, x2d)


def _pad_lanes(g, width):
    return jnp.pad(g, (0, width - g.shape[0]))[None, :]


def _layer(x, cosv, sinv, cost, sint, mem, norm_g, w_in, conv_w, w_conv_out, mla_q_norm_g, w_uq, mla_kv_norm_g,
           w_ukv, mla_qn_nope_g, mla_qn_rope_g, mla_kn_nope_g, mla_kn_rope_g, w_mla_out, mem_norm_g,
           w_mem_kv, mem_qn_g, mem_kn_g, w_mem_out, w_o):
    batch, seq, d = x.shape
    t = batch * seq
    x2d = x.reshape(t, d)

    o_conv, o_cq = 0, 4 * CONV_WIDTH
    o_mlaz = o_cq + Q_LORA + KV_LORA + QK_ROPE
    o_memq = o_mlaz + MLA_WIDTH
    o_gate = o_memq + 2 * MEM_WIDTH
    wb = w_in.astype(BF16)
    n_groups = CONV_WIDTH // CONV_GROUP
    w_conv_in = (wb[:, o_conv:o_cq].reshape(d, 4, n_groups, CONV_GROUP).transpose(0, 2, 1, 3)
                 .reshape(d, 4 * CONV_WIDTH))
    w_mla_in = jnp.pad(wb[:, o_cq:o_mlaz], ((0, 0), (0, 128 - QK_ROPE)))
    w_mlaz = wb[:, o_mlaz:o_memq]
    w_mem_in = wb[:, o_memq:o_gate]
    w_gates = wb[:, o_gate:]
    w_uqt = w_uq.astype(BF16).T
    w_ukv_h = w_ukv.astype(BF16).reshape(KV_LORA, MLA_HEADS, QK_NOPE + V_HEAD)
    w_kn = w_ukv_h[:, :, :QK_NOPE].reshape(KV_LORA, MLA_HEADS * QK_NOPE)
    w_vt = w_ukv_h[:, :, QK_NOPE:].reshape(KV_LORA, MLA_WIDTH).T
    qscale = (QK_NOPE + QK_ROPE) ** -0.5 * LOG2_E

    h = _rmsnorm(x2d, norm_g[None, :], tm=512)
    conv_act = _conv_branch(h, w_conv_in, conv_w, seq, tm=1024)
    qt, k, vt = _mla_prep(h, w_mla_in, w_uqt, w_kn, w_vt, mla_q_norm_g[None, :], mla_kv_norm_g[None, :],
                          (mla_qn_nope_g * qscale)[:, None], (mla_qn_rope_g * qscale)[:, None],
                          mla_kn_nope_g[None, :], _pad_lanes(mla_kn_rope_g, 128), cosv, sinv, cost, sint,
                          batch, seq, tm=256, tk=ATTN_TILE)
    sz_mla = _gate_proj(h, w_mlaz, True, tm=1024, tn=1024)
    gates = _gate_proj(h, w_gates, False, tm=1024, tn=1024)
    mla_act = _mla_attention(qt, k, vt, sz_mla.reshape(batch, seq, MLA_WIDTH), tq=ATTN_TILE, heads=2)
    k_mem, v_mem = _mem_kv(mem, mem_norm_g[None, :], w_mem_kv.astype(BF16), mem_kn_g[None, :])
    mem_act = _mem_attention(h, w_mem_in, k_mem, v_mem, mem_qn_g[None, :], batch, seq, tm=512)
    merged = _merge(conv_act, mla_act.reshape(t, MLA_WIDTH), mem_act, w_conv_out.astype(BF16),
                    w_mla_out.astype(BF16), w_mem_out.astype(BF16), gates, tm=1024, tn=512)
    out = _out_proj(merged, w_o.astype(BF16), x2d, tm=1024, tn=1024)
    return out.reshape(batch, seq, d)


def kernel(x, positions, mem, norm_g, w_in, conv_w, w_conv_out, mla_q_norm_g, w_uq, mla_kv_norm_g, w_ukv, mla_qn_nope_g, mla_qn_rope_g, mla_kn_nope_g, mla_kn_rope_g, w_mla_out, mem_norm_g, w_mem_kv, mem_qn_g, mem_kn_g, w_mem_out, w_o):
    batch, seq, _ = x.shape
    half = QK_ROPE // 2
    inv_freq = jnp.power(ROPE_THETA, -jnp.arange(half, dtype=F32) / half)
    ang = positions.astype(F32)[..., None] * inv_freq
    cos, sin = jnp.cos(ang), jnp.sin(ang)
    zeros = jnp.zeros((batch, seq, 128 - QK_ROPE), F32)
    cosv = jnp.concatenate([cos, cos, zeros], axis=-1).reshape(batch * seq, 128)
    sinv = jnp.concatenate([-sin, sin, zeros], axis=-1).reshape(batch * seq, 128)
    cost = cos.reshape(batch * seq, half).T
    sint = sin.reshape(batch * seq, half).T
    for l in range(norm_g.shape[0]):
        x = _layer(x, cosv, sinv, cost, sint, mem, norm_g[l], w_in[l], conv_w[l], w_conv_out[l], mla_q_norm_g[l],
                   w_uq[l], mla_kv_norm_g[l], w_ukv[l], mla_qn_nope_g[l], mla_qn_rope_g[l],
                   mla_kn_nope_g[l], mla_kn_rope_g[l], w_mla_out[l], mem_norm_g[l], w_mem_kv[l],
                   mem_qn_g[l], mem_kn_g[l], w_mem_out[l], w_o[l])
    return x
```

```python
import functools

import jax
import jax.numpy as jnp
from jax import lax
from jax.experimental import pallas as pl
from jax.experimental.pallas import tpu as pltpu

F32 = jnp.float32
BF16 = jnp.bfloat16

EPS = 1e-6
CHUNK = 64
CONV_WIDTH = 1024
CONV_K = 3
CONV_GROUP = 256
MLA_HEADS = 16
Q_LORA = 512
KV_LORA = 512
QK_NOPE = 128
QK_ROPE = 64
V_HEAD = 128
MLA_WIDTH = MLA_HEADS * V_HEAD
HEAD_PAD = 256
ROPE_THETA = 10000.0
MEM_HEADS = 4
MEM_HEAD_DIM = 256
MEM_WIDTH = MEM_HEADS * MEM_HEAD_DIM
NEG_BIG = -1e30
LOG2_E = 1.4426950408889634
ATTN_TILE = 512

VMEM_LIMIT_BYTES = 52 * 1024 * 1024


def _params(*sem):
    return pltpu.CompilerParams(dimension_semantics=sem, vmem_limit_bytes=VMEM_LIMIT_BYTES)


def _rms(x, width):
    ms = jnp.sum(x * x, axis=-1, keepdims=True) * (1.0 / width)
    return x * lax.rsqrt(ms + EPS)


def _sigmoid(x):
    return 0.5 * jnp.tanh(0.5 * x) + 0.5


def _dot(a, b):
    return jnp.dot(a, b, preferred_element_type=F32)


def _dot_nt(a, b):
    return lax.dot_general(a, b, (((1,), (1,)), ((), ())), preferred_element_type=F32)


def _rmsnorm_kernel(x_ref, g_ref, o_ref):
    x = x_ref[...]
    o_ref[...] = (_rms(x, x.shape[-1]) * g_ref[...]).astype(o_ref.dtype)


def _rmsnorm(x2d, g, tm):
    t, d = x2d.shape
    return pl.pallas_call(
        _rmsnorm_kernel,
        out_shape=jax.ShapeDtypeStruct((t, d), BF16),
        grid=(t // tm,),
        in_specs=[pl.BlockSpec((tm, d), lambda i: (i, 0)),
                  pl.BlockSpec((1, d), lambda i: (0, 0))],
        out_specs=pl.BlockSpec((tm, d), lambda i: (i, 0)),
        compiler_params=_params("arbitrary"),
        name="rmsnorm",
    )(x2d, g)


def _conv_kernel(h_ref, wc_ref, wb_ref, wu_ref, wz_ref, cw_ref, o_ref, carry_ref, buf_ref,
                 *, tm, tiles_per_seq):
    i = pl.program_id(0)
    g = pl.program_id(1)
    h = h_ref[...]
    b = _dot(h, wb_ref[...])
    z = _dot(h, wz_ref[...])
    cu = _dot(h, wc_ref[...]) * _dot(h, wu_ref[...])
    seq_start = (i % tiles_per_seq) == 0
    buf_ref[0:8, :] = jnp.where(seq_start, 0.0, carry_ref[g])
    buf_ref[8:8 + tm, :] = cu
    carry_ref[g] = cu[tm - 8:tm, :]
    cw = cw_ref[...]
    y = (buf_ref[6:6 + tm, :] * cw[0:1, :] + buf_ref[7:7 + tm, :] * cw[1:2, :] + cu * cw[2:3, :])
    o_ref[...] = (b * y * (z * _sigmoid(z))).astype(o_ref.dtype)


def _conv_branch(h, w_front, conv_w, seq, tm):
    t, d = h.shape
    n_groups = CONV_WIDTH // CONV_GROUP
    kern = functools.partial(_conv_kernel, tm=tm, tiles_per_seq=seq // tm)
    part = lambda k: pl.BlockSpec((d, CONV_GROUP), lambda i, g: (0, k * n_groups + g))
    return pl.pallas_call(
        kern,
        out_shape=jax.ShapeDtypeStruct((t, CONV_WIDTH), BF16),
        grid=(t // tm, n_groups),
        in_specs=[pl.BlockSpec((tm, d), lambda i, g: (i, 0)),
                  part(0), part(1), part(2), part(3),
                  pl.BlockSpec((CONV_K, CONV_GROUP), lambda i, g: (0, g))],
        out_specs=pl.BlockSpec((tm, CONV_GROUP), lambda i, g: (i, g)),
        scratch_shapes=[pltpu.VMEM((n_groups, 8, CONV_GROUP), F32),
                        pltpu.VMEM((tm + 8, CONV_GROUP), F32)],
        compiler_params=_params("arbitrary", "arbitrary"),
        name="conv_branch",
    )(h, w_front, w_front, w_front, w_front, conv_w)


def _rope_t(x, g, cost, sint):
    half = QK_ROPE // 2
    x = x * lax.rsqrt(jnp.sum(x * x, axis=0, keepdims=True) * (1.0 / QK_ROPE) + EPS) * g
    x1 = x[0:half, :]
    x2 = x[half:QK_ROPE, :]
    return x1 * cost - x2 * sint, x2 * cost + x1 * sint


def _mla_prep_kernel(h_ref, win_ref, wkrt_ref, wuqt_ref, wkn_ref, wvt_ref, gq_ref, gkv_ref, gqn_ref,
                     gqr_ref, gkn_ref, gkr_ref, cost_ref, sint_ref, qt_ref, k_ref, vt_ref):
    tm = h_ref.shape[0]
    h = h_ref[...]
    p = _dot(h, win_ref[...])
    cqn = (_rms(p[:, 0:Q_LORA], Q_LORA) * gq_ref[...]).astype(BF16)
    ckvn = (_rms(p[:, Q_LORA:Q_LORA + KV_LORA], KV_LORA) * gkv_ref[...]).astype(BF16)
    qt = _dot_nt(wuqt_ref[...], cqn)
    kn = _dot(ckvn, wkn_ref[...])
    vt = _dot_nt(wvt_ref[...], ckvn)
    krt = _dot_nt(wkrt_ref[...], h)

    cost = cost_ref[...]
    sint = sint_ref[...]
    kr1, kr2 = _rope_t(krt, jnp.broadcast_to(gkr_ref[...], (QK_ROPE, tm)), cost, sint)
    zeros_f = jnp.zeros((HEAD_PAD - QK_NOPE - QK_ROPE, tm), F32)
    k_rope = jnp.concatenate([kr1, kr2, zeros_f], axis=0).T.astype(BF16)

    gqn = jnp.broadcast_to(gqn_ref[...], (QK_NOPE, tm))
    gqr = jnp.broadcast_to(gqr_ref[...], (QK_ROPE, tm))
    gkn = gkn_ref[...]
    zeros = zeros_f.astype(BF16)
    for hd in range(MLA_HEADS):
        lo = hd * (QK_NOPE + QK_ROPE)
        qn = qt[lo:lo + QK_NOPE, :]
        qr = qt[lo + QK_NOPE:lo + QK_NOPE + QK_ROPE, :]
        qn = qn * lax.rsqrt(jnp.sum(qn * qn, axis=0, keepdims=True) * (1.0 / QK_NOPE) + EPS) * gqn
        r1, r2 = _rope_t(qr, gqr, cost, sint)
        half = QK_ROPE // 2
        qt_ref[0, hd, 0:QK_NOPE, :] = qn.astype(BF16)
        qt_ref[0, hd, QK_NOPE:QK_NOPE + half, :] = r1.astype(BF16)
        qt_ref[0, hd, QK_NOPE + half:QK_NOPE + QK_ROPE, :] = r2.astype(BF16)
        qt_ref[0, hd, QK_NOPE + QK_ROPE:HEAD_PAD, :] = zeros
        kh = kn[:, hd * QK_NOPE:(hd + 1) * QK_NOPE]
        k_ref[0, hd, :, 0:QK_NOPE] = (_rms(kh, QK_NOPE) * gkn).astype(BF16)
        k_ref[0, hd, :, QK_NOPE:HEAD_PAD] = k_rope
        vt_ref[0, hd, 0, :, :] = vt[hd * V_HEAD:(hd + 1) * V_HEAD, :].astype(BF16)


def _mla_prep(h, w_front, w_krt, w_uqt, w_kn, w_vt, gq, gkv, gqn, gqr, gkn, gkr, cost, sint,
              batch, seq, tm, tk):
    t, d = h.shape
    ns = seq // tm
    per_chunk = tk // tm
    lora = Q_LORA + KV_LORA
    const = lambda b, s: (0, 0)
    row = lambda b, s: (b * ns + s, 0)
    col = lambda b, s: (0, b * ns + s)
    qt_shape = jax.ShapeDtypeStruct((batch, MLA_HEADS, HEAD_PAD, seq), BF16)
    k_shape = jax.ShapeDtypeStruct((batch, MLA_HEADS, seq, HEAD_PAD), BF16)
    vt_shape = jax.ShapeDtypeStruct((batch, MLA_HEADS, seq // tk, V_HEAD, tk), BF16)
    full = lambda a: pl.BlockSpec(a.shape, const)
    return pl.pallas_call(
        _mla_prep_kernel,
        out_shape=(qt_shape, k_shape, vt_shape),
        grid=(batch, ns),
        in_specs=[pl.BlockSpec((tm, d), row),
                  pl.BlockSpec((d, lora), lambda b, s: (0, 4 * CONV_WIDTH // lora)),
                  full(w_krt), full(w_uqt), full(w_kn), full(w_vt),
                  full(gq), full(gkv), full(gqn), full(gqr), full(gkn), full(gkr),
                  pl.BlockSpec((QK_ROPE // 2, tm), col), pl.BlockSpec((QK_ROPE // 2, tm), col)],
        out_specs=(pl.BlockSpec((1, MLA_HEADS, HEAD_PAD, tm), lambda b, s: (b, 0, 0, s)),
                   pl.BlockSpec((1, MLA_HEADS, tm, HEAD_PAD), lambda b, s: (b, 0, s, 0)),
                   pl.BlockSpec((1, MLA_HEADS, 1, V_HEAD, tm),
                                lambda b, s: (b, 0, s // per_chunk, 0, s % per_chunk))),
        compiler_params=_params("arbitrary", "arbitrary"),
        name="mla_prep",
    )(h, w_front, w_krt, w_uqt, w_kn, w_vt, gq, gkv, gqn, gqr, gkn, gkr, cost, sint)


def _attn_kernel(qt_ref, k_ref, vt_ref, sz_ref, o_ref, acc_ref, m_ref, l_ref, s0_ref, s1_ref,
                 *, tq, heads):
    qi = pl.program_id(2)

    def scores(hh, ki, dst):
        start = pl.multiple_of(ki * tq, tq)
        dst[hh] = _dot(k_ref[0, hh, pl.ds(start, tq), :], qt_ref[0, hh])

    def update(hh, ki, src, diagonal):
        s = src[hh]
        if diagonal:
            kc = lax.broadcasted_iota(jnp.int32, s.shape, 0) // CHUNK
            qc = lax.broadcasted_iota(jnp.int32, s.shape, 1) // CHUNK
            s = jnp.where(kc <= qc, s, NEG_BIG)
        m = m_ref[hh]
        m_new = jnp.maximum(m, jnp.max(s, axis=0, keepdims=True))
        alpha = jnp.exp2(m - m_new)
        p = jnp.exp2(s - m_new)
        m_ref[hh] = m_new
        l_ref[hh] = alpha * l_ref[hh] + jnp.sum(p, axis=0, keepdims=True)
        acc_ref[hh] = alpha * acc_ref[hh] + _dot(vt_ref[0, hh, ki], p.astype(BF16))

    acc_ref[...] = jnp.zeros_like(acc_ref)
    l_ref[...] = jnp.zeros_like(l_ref)
    m_ref[...] = jnp.full(m_ref.shape, NEG_BIG, F32)
    for hh in range(heads):
        scores(hh, 0, s0_ref)

    def step(ki, cur, nxt):
        for hh in range(heads):
            scores(hh, ki + 1, nxt)
            update(hh, ki, cur, False)

    def pair(j, carry):
        step(2 * j, s0_ref, s1_ref)
        step(2 * j + 1, s1_ref, s0_ref)
        return carry

    def finish(src):
        for hh in range(heads):
            update(hh, qi, src, True)
            y = (acc_ref[hh] * (1.0 / l_ref[hh])).T
            sz = sz_ref[0, :, hh * V_HEAD:(hh + 1) * V_HEAD].astype(F32)
            o_ref[0, :, hh * V_HEAD:(hh + 1) * V_HEAD] = (y * sz).astype(o_ref.dtype)

    lax.fori_loop(0, qi // 2, pair, 0)

    @pl.when(qi % 2 == 0)
    def _():
        finish(s0_ref)

    @pl.when(qi % 2 == 1)
    def _():
        step(qi - 1, s0_ref, s1_ref)
        finish(s1_ref)


def _mla_attention(qt, k, vt, sz, tq, heads):
    batch, n_heads, seq, _ = k.shape
    kern = functools.partial(_attn_kernel, tq=tq, heads=heads)
    return pl.pallas_call(
        kern,
        out_shape=jax.ShapeDtypeStruct((batch, seq, MLA_WIDTH), BF16),
        grid=(batch, n_heads // heads, seq // tq),
        in_specs=[pl.BlockSpec((1, heads, HEAD_PAD, tq), lambda b, h, i: (b, h, 0, i)),
                  pl.BlockSpec((1, heads, seq, HEAD_PAD), lambda b, h, i: (b, h, 0, 0)),
                  pl.BlockSpec((1, heads, seq // tq, V_HEAD, tq), lambda b, h, i: (b, h, 0, 0, 0)),
                  pl.BlockSpec((1, tq, heads * V_HEAD), lambda b, h, i: (b, i, h))],
        out_specs=pl.BlockSpec((1, tq, heads * V_HEAD), lambda b, h, i: (b, i, h)),
        scratch_shapes=[pltpu.VMEM((heads, V_HEAD, tq), F32),
                        pltpu.VMEM((heads, 1, tq), F32),
                        pltpu.VMEM((heads, 1, tq), F32),
                        pltpu.VMEM((heads, tq, tq), F32),
                        pltpu.VMEM((heads, tq, tq), F32)],
        compiler_params=_params("arbitrary", "arbitrary", "arbitrary"),
        name="mla_attention",
    )(qt, k, vt, sz)


def _mem_kv_kernel(mem_ref, g_ref, w_ref, gk_ref, k_ref, v_ref):
    m = mem_ref[0]
    kv = _dot((_rms(m, m.shape[-1]) * g_ref[...]).astype(BF16), w_ref[...])
    gk = gk_ref[...]
    for hd in range(MEM_HEADS):
        lo = hd * MEM_HEAD_DIM
        k_ref[0, :, lo:lo + MEM_HEAD_DIM] = (_rms(kv[:, lo:lo + MEM_HEAD_DIM], MEM_HEAD_DIM) * gk).astype(BF16)
    v_ref[0] = kv[:, MEM_WIDTH:].astype(BF16)


def _mem_kv(mem, g, w, gk):
    batch, m, d = mem.shape
    const = lambda b: (0, 0)
    shape = jax.ShapeDtypeStruct((batch, m, MEM_WIDTH), BF16)
    return pl.pallas_call(
        _mem_kv_kernel,
        out_shape=(shape, shape),
        grid=(batch,),
        in_specs=[pl.BlockSpec((1, m, d), lambda b: (b, 0, 0)),
                  pl.BlockSpec(g.shape, const),
                  pl.BlockSpec(w.shape, const),
                  pl.BlockSpec(gk.shape, const)],
        out_specs=(pl.BlockSpec((1, m, MEM_WIDTH), lambda b: (b, 0, 0)),
                   pl.BlockSpec((1, m, MEM_WIDTH), lambda b: (b, 0, 0))),
        compiler_params=_params("arbitrary"),
        name="mem_kv",
    )(mem, g, w, gk)


def _mem_attn_kernel(h_ref, w_ref, k_ref, v_ref, gq_ref, o_ref):
    p = _dot(h_ref[...], w_ref[...])
    gq = gq_ref[...] * (MEM_HEAD_DIM ** -0.5)
    for hd in range(MEM_HEADS):
        lo = hd * MEM_HEAD_DIM
        q = (_rms(p[:, lo:lo + MEM_HEAD_DIM], MEM_HEAD_DIM) * gq).astype(BF16)
        z = p[:, MEM_WIDTH + lo:MEM_WIDTH + lo + MEM_HEAD_DIM]
        s = _dot_nt(q, k_ref[0, :, lo:lo + MEM_HEAD_DIM])
        e = jnp.exp(s - jnp.max(s, axis=-1, keepdims=True))
        y = _dot(e.astype(BF16), v_ref[0, :, lo:lo + MEM_HEAD_DIM])
        y = y * (1.0 / jnp.sum(e, axis=-1, keepdims=True))
        o_ref[:, lo:lo + MEM_HEAD_DIM] = (y * (z * _sigmoid(z))).astype(o_ref.dtype)


def _mem_attention(h, w, col0, k_mem, v_mem, gq, batch, seq, tm):
    t, d = h.shape
    ns = seq // tm
    m = k_mem.shape[1]
    const = lambda b, s: (0, 0)
    j0 = col0 // (2 * MEM_WIDTH)
    return pl.pallas_call(
        _mem_attn_kernel,
        out_shape=jax.ShapeDtypeStruct((t, MEM_WIDTH), BF16),
        grid=(batch, ns),
        in_specs=[pl.BlockSpec((tm, d), lambda b, s: (b * ns + s, 0)),
                  pl.BlockSpec((d, 2 * MEM_WIDTH), lambda b, s: (0, j0)),
                  pl.BlockSpec((1, m, MEM_WIDTH), lambda b, s: (b, 0, 0)),
                  pl.BlockSpec((1, m, MEM_WIDTH), lambda b, s: (b, 0, 0)),
                  pl.BlockSpec(gq.shape, const)],
        out_specs=pl.BlockSpec((tm, MEM_WIDTH), lambda b, s: (b * ns + s, 0)),
        compiler_params=_params("arbitrary", "arbitrary"),
        name="mem_attention",
    )(h, w, k_mem, v_mem, gq)


def _gate_kernel(h_ref, w_ref, o_ref, *, silu):
    p = _dot(h_ref[...], w_ref[...])
    s = _sigmoid(p)
    o_ref[...] = ((p * s) if silu else s).astype(o_ref.dtype)


def _gate_proj(h, w, col0, n, silu, tm, tn):
    t, d = h.shape
    j0 = col0 // tn
    return pl.pallas_call(
        functools.partial(_gate_kernel, silu=silu),
        out_shape=jax.ShapeDtypeStruct((t, n), BF16),
        grid=(t // tm, n // tn),
        in_specs=[pl.BlockSpec((tm, d), lambda i, j: (i, 0)),
                  pl.BlockSpec((d, tn), lambda i, j: (0, j0 + j))],
        out_specs=pl.BlockSpec((tm, tn), lambda i, j: (i, j)),
        compiler_params=_params("arbitrary", "arbitrary"),
        name="silu_gate_proj" if silu else "sigmoid_gate_proj",
    )(h, w)


def _merge_kernel(ac_ref, am_ref, ae_ref, wc_ref, wm_ref, we_ref, gc_ref, gm_ref, ge_ref, o_ref):
    merged = (gc_ref[...].astype(F32) * _dot(ac_ref[...], wc_ref[...])
              + gm_ref[...].astype(F32) * _dot(am_ref[...], wm_ref[...])
              + ge_ref[...].astype(F32) * _dot(ae_ref[...], we_ref[...]))
    o_ref[...] = merged.astype(o_ref.dtype)


def _merge(a_conv, a_mla, a_mem, w_conv, w_mla, w_mem, gates, tm, tn):
    t = a_conv.shape[0]
    n = w_conv.shape[1]
    nj = n // tn
    act = lambda a: pl.BlockSpec((tm, a.shape[1]), lambda i, j: (i, 0))
    wgt = lambda w: pl.BlockSpec((w.shape[0], tn), lambda i, j: (0, j))
    gate = lambda k: pl.BlockSpec((tm, tn), lambda i, j: (i, k * nj + j))
    return pl.pallas_call(
        _merge_kernel,
        out_shape=jax.ShapeDtypeStruct((t, n), BF16),
        grid=(t // tm, nj),
        in_specs=[act(a_conv), act(a_mla), act(a_mem), wgt(w_conv), wgt(w_mla), wgt(w_mem),
                  gate(0), gate(1), gate(2)],
        out_specs=pl.BlockSpec((tm, tn), lambda i, j: (i, j)),
        compiler_params=_params("arbitrary", "arbitrary"),
        name="merge",
    )(a_conv, a_mla, a_mem, w_conv, w_mla, w_mem, gates, gates, gates)


def _out_kernel(a_ref, w_ref, x_ref, o_ref):
    o_ref[...] = x_ref[...] + _dot(a_ref[...], w_ref[...])


def _out_proj(merged, w_o, x2d, tm, tn):
    t, d = merged.shape
    n = w_o.shape[1]
    return pl.pallas_call(
        _out_kernel,
        out_shape=jax.ShapeDtypeStruct((t, n), F32),
        grid=(t // tm, n // tn),
        in_specs=[pl.BlockSpec((tm, d), lambda i, j: (i, 0)),
                  pl.BlockSpec((d, tn), lambda i, j: (0, j)),
                  pl.BlockSpec((tm, tn), lambda i, j: (i, j))],
        out_specs=pl.BlockSpec((tm, tn), lambda i, j: (i, j)),
        compiler_params=_params("arbitrary", "arbitrary"),
        name="out_proj",
    )(merged, w_o, x2d)


def _layer(x, cost, sint, mem, norm_g, w_in, conv_w, w_conv_out, mla_q_norm_g, w_uq, mla_kv_norm_g,
           w_ukv, mla_qn_nope_g, mla_qn_rope_g, mla_kn_nope_g, mla_kn_rope_g, w_mla_out, mem_norm_g,
           w_mem_kv, mem_qn_g, mem_kn_g, w_mem_out, w_o):
    batch, seq, d = x.shape
    t = batch * seq
    x2d = x.reshape(t, d)

    o_kr = 4 * CONV_WIDTH + Q_LORA + KV_LORA
    w_front = w_in[:, :o_kr].astype(BF16)
    w_krt = w_in[:, o_kr:o_kr + QK_ROPE].T.astype(BF16)
    w_rest = w_in[:, o_kr + QK_ROPE:].astype(BF16)
    w_uqt = w_uq.astype(BF16).T
    w_ukv_h = w_ukv.astype(BF16).reshape(KV_LORA, MLA_HEADS, QK_NOPE + V_HEAD)
    w_kn = w_ukv_h[:, :, :QK_NOPE].reshape(KV_LORA, MLA_HEADS * QK_NOPE)
    w_vt = w_ukv_h[:, :, QK_NOPE:].reshape(KV_LORA, MLA_WIDTH).T
    qscale = (QK_NOPE + QK_ROPE) ** -0.5 * LOG2_E

    h = _rmsnorm(x2d, norm_g[None, :], tm=512)
    conv_act = _conv_branch(h, w_front, conv_w, seq, tm=1024)
    qt, k, vt = _mla_prep(h, w_front, w_krt, w_uqt, w_kn, w_vt, mla_q_norm_g[None, :],
                          mla_kv_norm_g[None, :], (mla_qn_nope_g * qscale)[:, None],
                          (mla_qn_rope_g * qscale)[:, None], mla_kn_nope_g[None, :],
                          mla_kn_rope_g[:, None], cost, sint, batch, seq, tm=256, tk=ATTN_TILE)
    sz_mla = _gate_proj(h, w_rest, 0, MLA_WIDTH, True, tm=1024, tn=2048)
    gates = _gate_proj(h, w_rest, MLA_WIDTH + 2 * MEM_WIDTH, 3 * d, False, tm=1024, tn=2048)
    mla_act = _mla_attention(qt, k, vt, sz_mla.reshape(batch, seq, MLA_WIDTH), tq=ATTN_TILE, heads=2)
    k_mem, v_mem = _mem_kv(mem, mem_norm_g[None, :], w_mem_kv.astype(BF16), mem_kn_g[None, :])
    mem_act = _mem_attention(h, w_rest, MLA_WIDTH, k_mem, v_mem, mem_qn_g[None, :], batch, seq, tm=512)
    merged = _merge(conv_act, mla_act.reshape(t, MLA_WIDTH), mem_act, w_conv_out.astype(BF16),
                    w_mla_out.astype(BF16), w_mem_out.astype(BF16), gates, tm=1024, tn=512)
    out = _out_proj(merged, w_o.astype(BF16), x2d, tm=1024, tn=1024)
    return out.reshape(batch, seq, d)


def kernel(x, positions, mem, norm_g, w_in, conv_w, w_conv_out, mla_q_norm_g, w_uq, mla_kv_norm_g, w_ukv, mla_qn_nope_g, mla_qn_rope_g, mla_kn_nope_g, mla_kn_rope_g, w_mla_out, mem_norm_g, w_mem_kv, mem_qn_g, mem_kn_g, w_mem_out, w_o):
    batch, seq, _ = x.shape
    half = QK_ROPE // 2
    inv_freq = jnp.power(ROPE_THETA, -jnp.arange(half, dtype=F32) / half)
    ang = positions.astype(F32)[..., None] * inv_freq
    cost = jnp.cos(ang).reshape(batch * seq, half).T
    sint = jnp.sin(ang).reshape(batch * seq, half).T
    for l in range(norm_g.shape[0]):
        x = _layer(x, cost, sint, mem, norm_g[l], w_in[l], conv_w[l], w_conv_out[l], mla_q_norm_g[l],
                   w_uq[l], mla_kv_norm_g[l], w_ukv[l], mla_qn_nope_g[l], mla_qn_rope_g[l],
                   mla_kn_nope_g[l], mla_kn_rope_g[l], w_mla_out[l], mem_norm_g[l], w_mem_kv[l],
                   mem_qn_g[l], mem_kn_g[l], w_mem_out[l], w_o[l])
    return x
```

```python
import functools

import jax
import jax.numpy as jnp
from jax import lax
from jax.experimental import pallas as pl
from jax.experimental.pallas import tpu as pltpu

F32 = jnp.float32
BF16 = jnp.bfloat16

EPS = 1e-6
CHUNK = 64
CONV_WIDTH = 1024
CONV_K = 3
CONV_GROUP = 256
MLA_HEADS = 16
Q_LORA = 512
KV_LORA = 512
QK_NOPE = 128
QK_ROPE = 64
V_HEAD = 128
MLA_WIDTH = MLA_HEADS * V_HEAD
HEAD_PAD = 256
ROPE_THETA = 10000.0
MEM_HEADS = 4
MEM_HEAD_DIM = 256
MEM_WIDTH = MEM_HEADS * MEM_HEAD_DIM
NEG_BIG = -1e30
LOG2_E = 1.4426950408889634
ATTN_TILE = 512

VMEM_LIMIT_BYTES = 52 * 1024 * 1024


def _params(*sem):
    return pltpu.CompilerParams(dimension_semantics=sem, vmem_limit_bytes=VMEM_LIMIT_BYTES)


def _rms(x, width):
    ms = jnp.sum(x * x, axis=-1, keepdims=True) * (1.0 / width)
    return x * lax.rsqrt(ms + EPS)


def _sigmoid(x):
    return 0.5 * jnp.tanh(0.5 * x) + 0.5


def _dot(a, b):
    return jnp.dot(a, b, preferred_element_type=F32)


def _dot_nt(a, b):
    return lax.dot_general(a, b, (((1,), (1,)), ((), ())), preferred_element_type=F32)


def _rmsnorm_kernel(x_ref, g_ref, o_ref):
    x = x_ref[...]
    o_ref[...] = (_rms(x, x.shape[-1]) * g_ref[...]).astype(o_ref.dtype)


def _rmsnorm(x2d, g, tm):
    t, d = x2d.shape
    return pl.pallas_call(
        _rmsnorm_kernel,
        out_shape=jax.ShapeDtypeStruct((t, d), BF16),
        grid=(t // tm,),
        in_specs=[pl.BlockSpec((tm, d), lambda i: (i, 0)),
                  pl.BlockSpec((1, d), lambda i: (0, 0))],
        out_specs=pl.BlockSpec((tm, d), lambda i: (i, 0)),
        compiler_params=_params("arbitrary"),
        name="rmsnorm",
    )(x2d, g)


def _conv_kernel(h_ref, wc_ref, wb_ref, wu_ref, wz_ref, cw_ref, o_ref, carry_ref, buf_ref,
                 *, tm, tiles_per_seq):
    i = pl.program_id(0)
    g = pl.program_id(1)
    h = h_ref[...]
    b = _dot_nt(h, wb_ref[...])
    z = _dot_nt(h, wz_ref[...])
    cu = _dot_nt(h, wc_ref[...]) * _dot_nt(h, wu_ref[...])
    seq_start = (i % tiles_per_seq) == 0
    buf_ref[0:8, :] = jnp.where(seq_start, 0.0, carry_ref[g])
    buf_ref[8:8 + tm, :] = cu
    carry_ref[g] = cu[tm - 8:tm, :]
    cw = cw_ref[...]
    y = (buf_ref[6:6 + tm, :] * cw[0:1, :] + buf_ref[7:7 + tm, :] * cw[1:2, :] + cu * cw[2:3, :])
    o_ref[...] = (b * y * (z * _sigmoid(z))).astype(o_ref.dtype)


def _conv_branch(h, w_front, conv_w, seq, tm):
    t, d = h.shape
    n_groups = CONV_WIDTH // CONV_GROUP
    kern = functools.partial(_conv_kernel, tm=tm, tiles_per_seq=seq // tm)
    part = lambda k: pl.BlockSpec((CONV_GROUP, d), lambda i, g: (k * n_groups + g, 0))
    return pl.pallas_call(
        kern,
        out_shape=jax.ShapeDtypeStruct((t, CONV_WIDTH), BF16),
        grid=(t // tm, n_groups),
        in_specs=[pl.BlockSpec((tm, d), lambda i, g: (i, 0)),
                  part(0), part(1), part(2), part(3),
                  pl.BlockSpec((CONV_K, CONV_GROUP), lambda i, g: (0, g))],
        out_specs=pl.BlockSpec((tm, CONV_GROUP), lambda i, g: (i, g)),
        scratch_shapes=[pltpu.VMEM((n_groups, 8, CONV_GROUP), F32),
                        pltpu.VMEM((tm + 8, CONV_GROUP), F32)],
        compiler_params=_params("arbitrary", "arbitrary"),
        name="conv_branch",
    )(h, w_front, w_front, w_front, w_front, conv_w)


def _rope_t(x, g, cost, sint):
    half = QK_ROPE // 2
    x = x * lax.rsqrt(jnp.sum(x * x, axis=0, keepdims=True) * (1.0 / QK_ROPE) + EPS) * g
    x1 = x[0:half, :]
    x2 = x[half:QK_ROPE, :]
    return x1 * cost - x2 * sint, x2 * cost + x1 * sint


def _mla_prep_kernel(h_ref, win_ref, wkrt_ref, wuqt_ref, wkn_ref, wvt_ref, gq_ref, gkv_ref, gqn_ref,
                     gqr_ref, gkn_ref, gkr_ref, cost_ref, sint_ref, qt_ref, k_ref, vt_ref):
    tm = h_ref.shape[0]
    h = h_ref[...]
    p = _dot_nt(h, win_ref[...])
    cqn = (_rms(p[:, 0:Q_LORA], Q_LORA) * gq_ref[...]).astype(BF16)
    ckvn = (_rms(p[:, Q_LORA:Q_LORA + KV_LORA], KV_LORA) * gkv_ref[...]).astype(BF16)
    qt = _dot_nt(wuqt_ref[...], cqn)
    kn = _dot(ckvn, wkn_ref[...])
    vt = _dot_nt(wvt_ref[...], ckvn)
    krt = _dot_nt(wkrt_ref[...], h)

    cost = cost_ref[...]
    sint = sint_ref[...]
    kr1, kr2 = _rope_t(krt, jnp.broadcast_to(gkr_ref[...], (QK_ROPE, tm)), cost, sint)
    zeros_f = jnp.zeros((HEAD_PAD - QK_NOPE - QK_ROPE, tm), F32)
    k_rope = jnp.concatenate([kr1, kr2, zeros_f], axis=0).T.astype(BF16)

    gqn = jnp.broadcast_to(gqn_ref[...], (QK_NOPE, tm))
    gqr = jnp.broadcast_to(gqr_ref[...], (QK_ROPE, tm))
    gkn = gkn_ref[...]
    zeros = zeros_f.astype(BF16)
    for hd in range(MLA_HEADS):
        lo = hd * (QK_NOPE + QK_ROPE)
        qn = qt[lo:lo + QK_NOPE, :]
        qr = qt[lo + QK_NOPE:lo + QK_NOPE + QK_ROPE, :]
        qn = qn * lax.rsqrt(jnp.sum(qn * qn, axis=0, keepdims=True) * (1.0 / QK_NOPE) + EPS) * gqn
        r1, r2 = _rope_t(qr, gqr, cost, sint)
        half = QK_ROPE // 2
        qt_ref[0, hd, 0:QK_NOPE, :] = qn.astype(BF16)
        qt_ref[0, hd, QK_NOPE:QK_NOPE + half, :] = r1.astype(BF16)
        qt_ref[0, hd, QK_NOPE + half:QK_NOPE + QK_ROPE, :] = r2.astype(BF16)
        qt_ref[0, hd, QK_NOPE + QK_ROPE:HEAD_PAD, :] = zeros
        kh = kn[:, hd * QK_NOPE:(hd + 1) * QK_NOPE]
        k_ref[0, hd, :, 0:QK_NOPE] = (_rms(kh, QK_NOPE) * gkn).astype(BF16)
        k_ref[0, hd, :, QK_NOPE:HEAD_PAD] = k_rope
        vt_ref[0, hd, 0, :, :] = vt[hd * V_HEAD:(hd + 1) * V_HEAD, :].astype(BF16)


def _mla_prep(h, w_front, w_uqt, w_kn, w_vt, gq, gkv, gqn, gqr, gkn, gkr, cost, sint,
              batch, seq, tm, tk):
    t, d = h.shape
    ns = seq // tm
    per_chunk = tk // tm
    lora = Q_LORA + KV_LORA
    const = lambda b, s: (0, 0)
    row = lambda b, s: (b * ns + s, 0)
    col = lambda b, s: (0, b * ns + s)
    qt_shape = jax.ShapeDtypeStruct((batch, MLA_HEADS, HEAD_PAD, seq), BF16)
    k_shape = jax.ShapeDtypeStruct((batch, MLA_HEADS, seq, HEAD_PAD), BF16)
    vt_shape = jax.ShapeDtypeStruct((batch, MLA_HEADS, seq // tk, V_HEAD, tk), BF16)
    full = lambda a: pl.BlockSpec(a.shape, const)
    return pl.pallas_call(
        _mla_prep_kernel,
        out_shape=(qt_shape, k_shape, vt_shape),
        grid=(batch, ns),
        in_specs=[pl.BlockSpec((tm, d), row),
                  pl.BlockSpec((lora, d), lambda b, s: (4 * CONV_WIDTH // lora, 0)),
                  pl.BlockSpec((QK_ROPE, d), lambda b, s: ((4 * CONV_WIDTH + lora) // QK_ROPE, 0)),
                  full(w_uqt), full(w_kn), full(w_vt),
                  full(gq), full(gkv), full(gqn), full(gqr), full(gkn), full(gkr),
                  pl.BlockSpec((QK_ROPE // 2, tm), col), pl.BlockSpec((QK_ROPE // 2, tm), col)],
        out_specs=(pl.BlockSpec((1, MLA_HEADS, HEAD_PAD, tm), lambda b, s: (b, 0, 0, s)),
                   pl.BlockSpec((1, MLA_HEADS, tm, HEAD_PAD), lambda b, s: (b, 0, s, 0)),
                   pl.BlockSpec((1, MLA_HEADS, 1, V_HEAD, tm),
                                lambda b, s: (b, 0, s // per_chunk, 0, s % per_chunk))),
        compiler_params=_params("arbitrary", "arbitrary"),
        name="mla_prep",
    )(h, w_front, w_front, w_uqt, w_kn, w_vt, gq, gkv, gqn, gqr, gkn, gkr, cost, sint)


def _attn_kernel(qt_ref, qtn_ref, k_ref, vt_ref, sz_ref, o_ref, acc_ref, m_ref, l_ref, s0_ref, s1_ref,
                 *, tq, heads):
    qi = pl.program_id(2)

    def scores(hh, ki, q_ref, dst):
        start = pl.multiple_of(ki * tq, tq)
        s = _dot(k_ref[0, hh, pl.ds(start, tq), :], q_ref[0, hh])
        dst[hh, 0:tq, :] = s
        dst[hh, tq:tq + 1, :] = jnp.max(s, axis=0, keepdims=True)

    def update(hh, ki, src, diagonal):
        s = src[hh, 0:tq, :]
        if diagonal:
            kc = lax.broadcasted_iota(jnp.int32, s.shape, 0) // CHUNK
            qc = lax.broadcasted_iota(jnp.int32, s.shape, 1) // CHUNK
            s = jnp.where(kc <= qc, s, NEG_BIG)
            m_tile = jnp.max(s, axis=0, keepdims=True)
        else:
            m_tile = src[hh, tq:tq + 1, :]
        m = m_ref[hh]
        m_new = jnp.maximum(m, m_tile)
        alpha = jnp.exp2(m - m_new)
        p = jnp.exp2(s - m_new)
        m_ref[hh] = m_new
        l_ref[hh] = alpha * l_ref[hh] + jnp.sum(p, axis=0, keepdims=True)
        acc_ref[hh] = alpha * acc_ref[hh] + _dot(vt_ref[0, hh, ki], p.astype(BF16))

    acc_ref[...] = jnp.zeros_like(acc_ref)
    l_ref[...] = jnp.zeros_like(l_ref)
    m_ref[...] = jnp.full(m_ref.shape, NEG_BIG, F32)

    @pl.when(qi == 0)
    def _():
        for hh in range(heads):
            scores(hh, 0, qt_ref, s0_ref)

    def step(ki, cur, nxt):
        for hh in range(heads):
            scores(hh, ki + 1, qt_ref, nxt)
            update(hh, ki, cur, False)

    def finish(diag, handoff):
        for hh in range(heads):
            scores(hh, 0, qtn_ref, handoff)
            update(hh, qi, diag, True)
            y = (acc_ref[hh] * (1.0 / l_ref[hh])).T
            sz = sz_ref[0, :, hh * V_HEAD:(hh + 1) * V_HEAD].astype(F32)
            o_ref[0, :, hh * V_HEAD:(hh + 1) * V_HEAD] = (y * sz).astype(o_ref.dtype)

    def run(cur, nxt):
        def pair(j, carry):
            step(2 * j, cur, nxt)
            step(2 * j + 1, nxt, cur)
            return carry

        lax.fori_loop(0, qi // 2, pair, 0)

        @pl.when(qi % 2 == 0)
        def _():
            finish(cur, nxt)

        @pl.when(qi % 2 == 1)
        def _():
            step(qi - 1, cur, nxt)
            finish(nxt, cur)

    first = ((qi + 1) // 2) % 2

    @pl.when(first == 0)
    def _():
        run(s0_ref, s1_ref)

    @pl.when(first == 1)
    def _():
        run(s1_ref, s0_ref)


def _mla_attention(qt, k, vt, sz, tq, heads):
    batch, n_heads, seq, _ = k.shape
    kern = functools.partial(_attn_kernel, tq=tq, heads=heads)
    last = seq // tq - 1
    return pl.pallas_call(
        kern,
        out_shape=jax.ShapeDtypeStruct((batch, seq, MLA_WIDTH), BF16),
        grid=(batch, n_heads // heads, seq // tq),
        in_specs=[pl.BlockSpec((1, heads, HEAD_PAD, tq), lambda b, h, i: (b, h, 0, i)),
                  pl.BlockSpec((1, heads, HEAD_PAD, tq), lambda b, h, i: (b, h, 0, jnp.minimum(i + 1, last))),
                  pl.BlockSpec((1, heads, seq, HEAD_PAD), lambda b, h, i: (b, h, 0, 0)),
                  pl.BlockSpec((1, heads, seq // tq, V_HEAD, tq), lambda b, h, i: (b, h, 0, 0, 0)),
                  pl.BlockSpec((1, tq, heads * V_HEAD), lambda b, h, i: (b, i, h))],
        out_specs=pl.BlockSpec((1, tq, heads * V_HEAD), lambda b, h, i: (b, i, h)),
        scratch_shapes=[pltpu.VMEM((heads, V_HEAD, tq), F32),
                        pltpu.VMEM((heads, 1, tq), F32),
                        pltpu.VMEM((heads, 1, tq), F32),
                        pltpu.VMEM((heads, tq + 8, tq), F32),
                        pltpu.VMEM((heads, tq + 8, tq), F32)],
        compiler_params=_params("arbitrary", "arbitrary", "arbitrary"),
        name="mla_attention",
    )(qt, qt, k, vt, sz)


def _mem_kv_kernel(mem_ref, g_ref, w_ref, gk_ref, k_ref, v_ref):
    m = mem_ref[0]
    kv = _dot((_rms(m, m.shape[-1]) * g_ref[...]).astype(BF16), w_ref[...])
    gk = gk_ref[...]
    for hd in range(MEM_HEADS):
        lo = hd * MEM_HEAD_DIM
        k_ref[0, :, lo:lo + MEM_HEAD_DIM] = (_rms(kv[:, lo:lo + MEM_HEAD_DIM], MEM_HEAD_DIM) * gk).astype(BF16)
    v_ref[0] = kv[:, MEM_WIDTH:].astype(BF16)


def _mem_kv(mem, g, w, gk):
    batch, m, d = mem.shape
    const = lambda b: (0, 0)
    shape = jax.ShapeDtypeStruct((batch, m, MEM_WIDTH), BF16)
    return pl.pallas_call(
        _mem_kv_kernel,
        out_shape=(shape, shape),
        grid=(batch,),
        in_specs=[pl.BlockSpec((1, m, d), lambda b: (b, 0, 0)),
                  pl.BlockSpec(g.shape, const),
                  pl.BlockSpec(w.shape, const),
                  pl.BlockSpec(gk.shape, const)],
        out_specs=(pl.BlockSpec((1, m, MEM_WIDTH), lambda b: (b, 0, 0)),
                   pl.BlockSpec((1, m, MEM_WIDTH), lambda b: (b, 0, 0))),
        compiler_params=_params("arbitrary"),
        name="mem_kv",
    )(mem, g, w, gk)


def _mem_attn_kernel(h_ref, w_ref, k_ref, v_ref, gq_ref, o_ref):
    p = _dot_nt(h_ref[...], w_ref[...])
    gq = gq_ref[...] * (MEM_HEAD_DIM ** -0.5)
    for hd in range(MEM_HEADS):
        lo = hd * MEM_HEAD_DIM
        q = (_rms(p[:, lo:lo + MEM_HEAD_DIM], MEM_HEAD_DIM) * gq).astype(BF16)
        z = p[:, MEM_WIDTH + lo:MEM_WIDTH + lo + MEM_HEAD_DIM]
        s = _dot_nt(q, k_ref[0, :, lo:lo + MEM_HEAD_DIM])
        e = jnp.exp(s - jnp.max(s, axis=-1, keepdims=True))
        y = _dot(e.astype(BF16), v_ref[0, :, lo:lo + MEM_HEAD_DIM])
        y = y * (1.0 / jnp.sum(e, axis=-1, keepdims=True))
        o_ref[:, lo:lo + MEM_HEAD_DIM] = (y * (z * _sigmoid(z))).astype(o_ref.dtype)


def _mem_attention(h, w, col0, k_mem, v_mem, gq, batch, seq, tm):
    t, d = h.shape
    ns = seq // tm
    m = k_mem.shape[1]
    const = lambda b, s: (0, 0)
    j0 = col0 // (2 * MEM_WIDTH)
    return pl.pallas_call(
        _mem_attn_kernel,
        out_shape=jax.ShapeDtypeStruct((t, MEM_WIDTH), BF16),
        grid=(batch, ns),
        in_specs=[pl.BlockSpec((tm, d), lambda b, s: (b * ns + s, 0)),
                  pl.BlockSpec((2 * MEM_WIDTH, d), lambda b, s: (j0, 0)),
                  pl.BlockSpec((1, m, MEM_WIDTH), lambda b, s: (b, 0, 0)),
                  pl.BlockSpec((1, m, MEM_WIDTH), lambda b, s: (b, 0, 0)),
                  pl.BlockSpec(gq.shape, const)],
        out_specs=pl.BlockSpec((tm, MEM_WIDTH), lambda b, s: (b * ns + s, 0)),
        compiler_params=_params("arbitrary", "arbitrary"),
        name="mem_attention",
    )(h, w, k_mem, v_mem, gq)


def _gate_kernel(h_ref, w_ref, o_ref, *, silu):
    p = _dot_nt(h_ref[...], w_ref[...])
    s = _sigmoid(p)
    o_ref[...] = ((p * s) if silu else s).astype(o_ref.dtype)


def _gate_proj(h, w, col0, n, silu, tm, tn):
    t, d = h.shape
    j0 = col0 // tn
    return pl.pallas_call(
        functools.partial(_gate_kernel, silu=silu),
        out_shape=jax.ShapeDtypeStruct((t, n), BF16),
        grid=(t // tm, n // tn),
        in_specs=[pl.BlockSpec((tm, d), lambda i, j: (i, 0)),
                  pl.BlockSpec((tn, d), lambda i, j: (j0 + j, 0))],
        out_specs=pl.BlockSpec((tm, tn), lambda i, j: (i, j)),
        compiler_params=_params("arbitrary", "arbitrary"),
        name="silu_gate_proj" if silu else "sigmoid_gate_proj",
    )(h, w)


def _merge_kernel(ac_ref, am_ref, ae_ref, wc_ref, wm_ref, we_ref, gc_ref, gm_ref, ge_ref, o_ref):
    merged = (gc_ref[...].astype(F32) * _dot(ac_ref[...], wc_ref[...])
              + gm_ref[...].astype(F32) * _dot(am_ref[...], wm_ref[...])
              + ge_ref[...].astype(F32) * _dot(ae_ref[...], we_ref[...]))
    o_ref[...] = merged.astype(o_ref.dtype)


def _merge(a_conv, a_mla, a_mem, w_conv, w_mla, w_mem, gates, tm, tn):
    t = a_conv.shape[0]
    n = w_conv.shape[1]
    nj = n // tn
    act = lambda a: pl.BlockSpec((tm, a.shape[1]), lambda i, j: (i, 0))
    wgt = lambda w: pl.BlockSpec((w.shape[0], tn), lambda i, j: (0, j))
    gate = lambda k: pl.BlockSpec((tm, tn), lambda i, j: (i, k * nj + j))
    return pl.pallas_call(
        _merge_kernel,
        out_shape=jax.ShapeDtypeStruct((t, n), BF16),
        grid=(t // tm, nj),
        in_specs=[act(a_conv), act(a_mla), act(a_mem), wgt(w_conv), wgt(w_mla), wgt(w_mem),
                  gate(0), gate(1), gate(2)],
        out_specs=pl.BlockSpec((tm, tn), lambda i, j: (i, j)),
        compiler_params=_params("arbitrary", "arbitrary"),
        name="merge",
    )(a_conv, a_mla, a_mem, w_conv, w_mla, w_mem, gates, gates, gates)


def _out_kernel(a_ref, w_ref, x_ref, o_ref):
    o_ref[...] = x_ref[...] + _dot(a_ref[...], w_ref[...])


def _out_proj(merged, w_o, x2d, tm, tn):
    t, d = merged.shape
    n = w_o.shape[1]
    return pl.pallas_call(
        _out_kernel,
        out_shape=jax.ShapeDtypeStruct((t, n), F32),
        grid=(t // tm, n // tn),
        in_specs=[pl.BlockSpec((tm, d), lambda i, j: (i, 0)),
                  pl.BlockSpec((d, tn), lambda i, j: (0, j)),
                  pl.BlockSpec((tm, tn), lambda i, j: (i, j))],
        out_specs=pl.BlockSpec((tm, tn), lambda i, j: (i, j)),
        compiler_params=_params("arbitrary", "arbitrary"),
        name="out_proj",
    )(merged, w_o, x2d)


def _layer(x, cost, sint, mem, norm_g, w_in, conv_w, w_conv_out, mla_q_norm_g, w_uq, mla_kv_norm_g,
           w_ukv, mla_qn_nope_g, mla_qn_rope_g, mla_kn_nope_g, mla_kn_rope_g, w_mla_out, mem_norm_g,
           w_mem_kv, mem_qn_g, mem_kn_g, w_mem_out, w_o):
    batch, seq, d = x.shape
    t = batch * seq
    x2d = x.reshape(t, d)

    o_rest = 4 * CONV_WIDTH + Q_LORA + KV_LORA + QK_ROPE
    w_front = w_in.T.astype(BF16)
    w_rest = w_front[o_rest:]
    w_uqt = w_uq.astype(BF16).T
    w_ukv_h = w_ukv.astype(BF16).reshape(KV_LORA, MLA_HEADS, QK_NOPE + V_HEAD)
    w_kn = w_ukv_h[:, :, :QK_NOPE].reshape(KV_LORA, MLA_HEADS * QK_NOPE)
    w_vt = w_ukv_h[:, :, QK_NOPE:].reshape(KV_LORA, MLA_WIDTH).T
    qscale = (QK_NOPE + QK_ROPE) ** -0.5 * LOG2_E

    h = _rmsnorm(x2d, norm_g[None, :], tm=512)
    conv_act = _conv_branch(h, w_front, conv_w, seq, tm=1024)
    qt, k, vt = _mla_prep(h, w_front, w_uqt, w_kn, w_vt, mla_q_norm_g[None, :],
                          mla_kv_norm_g[None, :], (mla_qn_nope_g * qscale)[:, None],
                          (mla_qn_rope_g * qscale)[:, None], mla_kn_nope_g[None, :],
                          mla_kn_rope_g[:, None], cost, sint, batch, seq, tm=256, tk=ATTN_TILE)
    sz_mla = _gate_proj(h, w_rest, 0, MLA_WIDTH, True, tm=1024, tn=2048)
    gates = _gate_proj(h, w_rest, MLA_WIDTH + 2 * MEM_WIDTH, 3 * d, False, tm=1024, tn=2048)
    mla_act = _mla_attention(qt, k, vt, sz_mla.reshape(batch, seq, MLA_WIDTH), tq=ATTN_TILE, heads=2)
    k_mem, v_mem = _mem_kv(mem, mem_norm_g[None, :], w_mem_kv.astype(BF16), mem_kn_g[None, :])
    mem_act = _mem_attention(h, w_rest, MLA_WIDTH, k_mem, v_mem, mem_qn_g[None, :], batch, seq, tm=512)
    merged = _merge(conv_act, mla_act.reshape(t, MLA_WIDTH), mem_act, w_conv_out.astype(BF16),
                    w_mla_out.astype(BF16), w_mem_out.astype(BF16), gates, tm=1024, tn=512)
    out = _out_proj(merged, w_o.astype(BF16), x2d, tm=1024, tn=1024)
    return out.reshape(batch, seq, d)


def kernel(x, positions, mem, norm_g, w_in, conv_w, w_conv_out, mla_q_norm_g, w_uq, mla_kv_norm_g, w_ukv, mla_qn_nope_g, mla_qn_rope_g, mla_kn_nope_g, mla_kn_rope_g, w_mla_out, mem_norm_g, w_mem_kv, mem_qn_g, mem_kn_g, w_mem_out, w_o):
    batch, seq, _ = x.shape
    half = QK_ROPE // 2
    inv_freq = jnp.power(ROPE_THETA, -jnp.arange(half, dtype=F32) / half)
    ang = positions.astype(F32)[..., None] * inv_freq
    cost = jnp.cos(ang).reshape(batch * seq, half).T
    sint = jnp.sin(ang).reshape(batch * seq, half).T
    for l in range(norm_g.shape[0]):
        x = _layer(x, cost, sint, mem, norm_g[l], w_in[l], conv_w[l], w_conv_out[l], mla_q_norm_g[l],
                   w_uq[l], mla_kv_norm_g[l], w_ukv[l], mla_qn_nope_g[l], mla_qn_rope_g[l],
                   mla_kn_nope_g[l], mla_kn_rope_g[l], w_mla_out[l], mem_norm_g[l], w_mem_kv[l],
                   mem_qn_g[l], mem_kn_g[l], w_mem_out[l], w_o[l])
    return x
```

```python
import functools

import jax
import jax.numpy as jnp
from jax import lax
from jax.experimental import pallas as pl
from jax.experimental.pallas import tpu as pltpu

F32 = jnp.float32
BF16 = jnp.bfloat16

EPS = 1e-6
CHUNK = 64
CONV_WIDTH = 1024
CONV_K = 3
CONV_GROUP = 256
MLA_HEADS = 16
Q_LORA = 512
KV_LORA = 512
QK_NOPE = 128
QK_ROPE = 64
V_HEAD = 128
MLA_WIDTH = MLA_HEADS * V_HEAD
HEAD_PAD = 256
ROPE_THETA = 10000.0
MEM_HEADS = 4
MEM_HEAD_DIM = 256
MEM_WIDTH = MEM_HEADS * MEM_HEAD_DIM
NEG_BIG = -1e30
LOG2_E = 1.4426950408889634
ATTN_TILE = 512

VMEM_LIMIT_BYTES = 52 * 1024 * 1024
MERGE_VMEM_LIMIT_BYTES = 50 * 1024 * 1024


def _params(*sem):
    return pltpu.CompilerParams(dimension_semantics=sem, vmem_limit_bytes=VMEM_LIMIT_BYTES)


def _rms(x, width):
    ms = jnp.sum(x * x, axis=-1, keepdims=True) * (1.0 / width)
    return x * lax.rsqrt(ms + EPS)


def _sigmoid(x):
    return 0.5 * jnp.tanh(0.5 * x) + 0.5


def _dot(a, b):
    return jnp.dot(a, b, preferred_element_type=F32)


def _dot_nt(a, b):
    return lax.dot_general(a, b, (((1,), (1,)), ((), ())), preferred_element_type=F32)


def _conv_kernel(x_ref, ng_ref, wc_ref, wb_ref, wu_ref, wz_ref, cw_ref, o_ref, h_ref, hs_ref,
                 carry_ref, buf_ref, *, tm, tiles_per_seq):
    i = pl.program_id(0)
    g = pl.program_id(1)

    @pl.when(g == 0)
    def _():
        x = x_ref[...]
        hn = (_rms(x, x.shape[-1]) * ng_ref[...]).astype(BF16)
        hs_ref[...] = hn
        h_ref[...] = hn

    h = hs_ref[...]
    b = _dot_nt(h, wb_ref[...])
    z = _dot_nt(h, wz_ref[...])
    cu = _dot_nt(h, wc_ref[...]) * _dot_nt(h, wu_ref[...])
    seq_start = (i % tiles_per_seq) == 0
    buf_ref[0:8, :] = jnp.where(seq_start, 0.0, carry_ref[g])
    buf_ref[8:8 + tm, :] = cu
    carry_ref[g] = cu[tm - 8:tm, :]
    cw = cw_ref[...]
    y = (buf_ref[6:6 + tm, :] * cw[0:1, :] + buf_ref[7:7 + tm, :] * cw[1:2, :] + cu * cw[2:3, :])
    o_ref[...] = (b * y * (z * _sigmoid(z))).astype(o_ref.dtype)


def _norm_conv_branch(x2d, norm_g, w_front, conv_w, seq, tm):
    t, d = x2d.shape
    n_groups = CONV_WIDTH // CONV_GROUP
    kern = functools.partial(_conv_kernel, tm=tm, tiles_per_seq=seq // tm)
    part = lambda k: pl.BlockSpec((CONV_GROUP, d), lambda i, g: (k * n_groups + g, 0))
    return pl.pallas_call(
        kern,
        out_shape=(jax.ShapeDtypeStruct((t, CONV_WIDTH), BF16), jax.ShapeDtypeStruct((t, d), BF16)),
        grid=(t // tm, n_groups),
        in_specs=[pl.BlockSpec((tm, d), lambda i, g: (i, 0)),
                  pl.BlockSpec((1, d), lambda i, g: (0, 0)),
                  part(0), part(1), part(2), part(3),
                  pl.BlockSpec((CONV_K, CONV_GROUP), lambda i, g: (0, g))],
        out_specs=(pl.BlockSpec((tm, CONV_GROUP), lambda i, g: (i, g)),
                   pl.BlockSpec((tm, d), lambda i, g: (i, 0))),
        scratch_shapes=[pltpu.VMEM((tm, d), BF16),
                        pltpu.VMEM((n_groups, 8, CONV_GROUP), F32),
                        pltpu.VMEM((tm + 8, CONV_GROUP), F32)],
        compiler_params=_params("arbitrary", "arbitrary"),
        name="norm_conv_branch",
    )(x2d, norm_g, w_front, w_front, w_front, w_front, conv_w)


def _rope_t(x, g, cost, sint):
    half = QK_ROPE // 2
    x = x * lax.rsqrt(jnp.sum(x * x, axis=0, keepdims=True) * (1.0 / QK_ROPE) + EPS) * g
    x1 = x[0:half, :]
    x2 = x[half:QK_ROPE, :]
    return x1 * cost - x2 * sint, x2 * cost + x1 * sint


def _mla_prep_kernel(h_ref, win_ref, wkrt_ref, wuqt_ref, wkn_ref, wvt_ref, gq_ref, gkv_ref, gqn_ref,
                     gqr_ref, gkn_ref, gkr_ref, cost_ref, sint_ref, qt_ref, k_ref, vt_ref):
    tm = h_ref.shape[0]
    h = h_ref[...]
    p = _dot_nt(h, win_ref[...])
    cqn = (_rms(p[:, 0:Q_LORA], Q_LORA) * gq_ref[...]).astype(BF16)
    ckvn = (_rms(p[:, Q_LORA:Q_LORA + KV_LORA], KV_LORA) * gkv_ref[...]).astype(BF16)
    qt = _dot_nt(wuqt_ref[...], cqn)
    kn = _dot(ckvn, wkn_ref[...])
    vt = _dot_nt(wvt_ref[...], ckvn)
    krt = _dot_nt(wkrt_ref[...], h)

    cost = cost_ref[...]
    sint = sint_ref[...]
    kr1, kr2 = _rope_t(krt, jnp.broadcast_to(gkr_ref[...], (QK_ROPE, tm)), cost, sint)
    zeros_f = jnp.zeros((HEAD_PAD - QK_NOPE - QK_ROPE, tm), F32)
    k_rope = jnp.concatenate([kr1, kr2, zeros_f], axis=0).T.astype(BF16)

    gqn = jnp.broadcast_to(gqn_ref[...], (QK_NOPE, tm))
    gqr = jnp.broadcast_to(gqr_ref[...], (QK_ROPE, tm))
    gkn = gkn_ref[...]
    zeros = zeros_f.astype(BF16)
    for hd in range(MLA_HEADS):
        lo = hd * (QK_NOPE + QK_ROPE)
        qn = qt[lo:lo + QK_NOPE, :]
        qr = qt[lo + QK_NOPE:lo + QK_NOPE + QK_ROPE, :]
        qn = qn * lax.rsqrt(jnp.sum(qn * qn, axis=0, keepdims=True) * (1.0 / QK_NOPE) + EPS) * gqn
        r1, r2 = _rope_t(qr, gqr, cost, sint)
        half = QK_ROPE // 2
        qt_ref[0, hd, 0:QK_NOPE, :] = qn.astype(BF16)
        qt_ref[0, hd, QK_NOPE:QK_NOPE + half, :] = r1.astype(BF16)
        qt_ref[0, hd, QK_NOPE + half:QK_NOPE + QK_ROPE, :] = r2.astype(BF16)
        qt_ref[0, hd, QK_NOPE + QK_ROPE:HEAD_PAD, :] = zeros
        kh = kn[:, hd * QK_NOPE:(hd + 1) * QK_NOPE]
        k_ref[0, hd, :, 0:QK_NOPE] = (_rms(kh, QK_NOPE) * gkn).astype(BF16)
        k_ref[0, hd, :, QK_NOPE:HEAD_PAD] = k_rope
        vt_ref[0, hd, 0, :, :] = vt[hd * V_HEAD:(hd + 1) * V_HEAD, :].astype(BF16)


def _mla_prep(h, w_front, w_uqt, w_kn, w_vt, gq, gkv, gqn, gqr, gkn, gkr, cost, sint,
              batch, seq, tm, tk):
    t, d = h.shape
    ns = seq // tm
    per_chunk = tk // tm
    lora = Q_LORA + KV_LORA
    const = lambda b, s: (0, 0)
    row = lambda b, s: (b * ns + s, 0)
    col = lambda b, s: (0, b * ns + s)
    qt_shape = jax.ShapeDtypeStruct((batch, MLA_HEADS, HEAD_PAD, seq), BF16)
    k_shape = jax.ShapeDtypeStruct((batch, MLA_HEADS, seq, HEAD_PAD), BF16)
    vt_shape = jax.ShapeDtypeStruct((batch, MLA_HEADS, seq // tk, V_HEAD, tk), BF16)
    full = lambda a: pl.BlockSpec(a.shape, const)
    return pl.pallas_call(
        _mla_prep_kernel,
        out_shape=(qt_shape, k_shape, vt_shape),
        grid=(batch, ns),
        in_specs=[pl.BlockSpec((tm, d), row),
                  pl.BlockSpec((lora, d), lambda b, s: (4 * CONV_WIDTH // lora, 0)),
                  pl.BlockSpec((QK_ROPE, d), lambda b, s: ((4 * CONV_WIDTH + lora) // QK_ROPE, 0)),
                  full(w_uqt), full(w_kn), full(w_vt),
                  full(gq), full(gkv), full(gqn), full(gqr), full(gkn), full(gkr),
                  pl.BlockSpec((QK_ROPE // 2, tm), col), pl.BlockSpec((QK_ROPE // 2, tm), col)],
        out_specs=(pl.BlockSpec((1, MLA_HEADS, HEAD_PAD, tm), lambda b, s: (b, 0, 0, s)),
                   pl.BlockSpec((1, MLA_HEADS, tm, HEAD_PAD), lambda b, s: (b, 0, s, 0)),
                   pl.BlockSpec((1, MLA_HEADS, 1, V_HEAD, tm),
                                lambda b, s: (b, 0, s // per_chunk, 0, s % per_chunk))),
        compiler_params=_params("arbitrary", "arbitrary"),
        name="mla_prep",
    )(h, w_front, w_front, w_uqt, w_kn, w_vt, gq, gkv, gqn, gqr, gkn, gkr, cost, sint)


def _attn_kernel(qt_ref, qtn_ref, k_ref, vt_ref, sz_ref, o_ref, acc_ref, m_ref, l_ref, s0_ref, s1_ref,
                 *, tq, heads):
    qi = pl.program_id(2)

    def scores(hh, ki, q_ref, dst):
        start = pl.multiple_of(ki * tq, tq)
        s = _dot(k_ref[0, hh, pl.ds(start, tq), :], q_ref[0, hh])
        dst[hh, 0:tq, :] = s
        dst[hh, tq:tq + 1, :] = jnp.max(s, axis=0, keepdims=True)

    def update(hh, ki, src, diagonal):
        s = src[hh, 0:tq, :]
        if diagonal:
            kc = lax.broadcasted_iota(jnp.int32, s.shape, 0) // CHUNK
            qc = lax.broadcasted_iota(jnp.int32, s.shape, 1) // CHUNK
            s = jnp.where(kc <= qc, s, NEG_BIG)
            m_tile = jnp.max(s, axis=0, keepdims=True)
        else:
            m_tile = src[hh, tq:tq + 1, :]
        m = m_ref[hh]
        m_new = jnp.maximum(m, m_tile)
        alpha = jnp.exp2(m - m_new)
        p = jnp.exp2(s - m_new)
        m_ref[hh] = m_new
        l_ref[hh] = alpha * l_ref[hh] + jnp.sum(p, axis=0, keepdims=True)
        acc_ref[hh] = alpha * acc_ref[hh] + _dot(vt_ref[0, hh, ki], p.astype(BF16))

    acc_ref[...] = jnp.zeros_like(acc_ref)
    l_ref[...] = jnp.zeros_like(l_ref)
    m_ref[...] = jnp.full(m_ref.shape, NEG_BIG, F32)

    @pl.when(qi == 0)
    def _():
        for hh in range(heads):
            scores(hh, 0, qt_ref, s0_ref)

    def step(ki, cur, nxt):
        for hh in range(heads):
            scores(hh, ki + 1, qt_ref, nxt)
            update(hh, ki, cur, False)

    def finish(diag, handoff):
        for hh in range(heads):
            scores(hh, 0, qtn_ref, handoff)
            update(hh, qi, diag, True)
            y = (acc_ref[hh] * (1.0 / l_ref[hh])).T
            sz = sz_ref[0, :, hh * V_HEAD:(hh + 1) * V_HEAD].astype(F32)
            o_ref[0, :, hh * V_HEAD:(hh + 1) * V_HEAD] = (y * sz).astype(o_ref.dtype)

    def run(cur, nxt):
        def pair(j, carry):
            step(2 * j, cur, nxt)
            step(2 * j + 1, nxt, cur)
            return carry

        lax.fori_loop(0, qi // 2, pair, 0)

        @pl.when(qi % 2 == 0)
        def _():
            finish(cur, nxt)

        @pl.when(qi % 2 == 1)
        def _():
            step(qi - 1, cur, nxt)
            finish(nxt, cur)

    first = ((qi + 1) // 2) % 2

    @pl.when(first == 0)
    def _():
        run(s0_ref, s1_ref)

    @pl.when(first == 1)
    def _():
        run(s1_ref, s0_ref)


def _mla_attention(qt, k, vt, sz, tq, heads):
    batch, n_heads, seq, _ = k.shape
    kern = functools.partial(_attn_kernel, tq=tq, heads=heads)
    last = seq // tq - 1
    return pl.pallas_call(
        kern,
        out_shape=jax.ShapeDtypeStruct((batch, seq, MLA_WIDTH), BF16),
        grid=(batch, n_heads // heads, seq // tq),
        in_specs=[pl.BlockSpec((1, heads, HEAD_PAD, tq), lambda b, h, i: (b, h, 0, i)),
                  pl.BlockSpec((1, heads, HEAD_PAD, tq), lambda b, h, i: (b, h, 0, jnp.minimum(i + 1, last))),
                  pl.BlockSpec((1, heads, seq, HEAD_PAD), lambda b, h, i: (b, h, 0, 0)),
                  pl.BlockSpec((1, heads, seq // tq, V_HEAD, tq), lambda b, h, i: (b, h, 0, 0, 0)),
                  pl.BlockSpec((1, tq, heads * V_HEAD), lambda b, h, i: (b, i, h))],
        out_specs=pl.BlockSpec((1, tq, heads * V_HEAD), lambda b, h, i: (b, i, h)),
        scratch_shapes=[pltpu.VMEM((heads, V_HEAD, tq), F32),
                        pltpu.VMEM((heads, 1, tq), F32),
                        pltpu.VMEM((heads, 1, tq), F32),
                        pltpu.VMEM((heads, tq + 8, tq), F32),
                        pltpu.VMEM((heads, tq + 8, tq), F32)],
        compiler_params=_params("arbitrary", "arbitrary", "arbitrary"),
        name="mla_attention",
    )(qt, qt, k, vt, sz)


def _mem_kv_kernel(mem_ref, g_ref, w_ref, gk_ref, k_ref, v_ref):
    m = mem_ref[0]
    kv = _dot((_rms(m, m.shape[-1]) * g_ref[...]).astype(BF16), w_ref[...])
    gk = gk_ref[...]
    for hd in range(MEM_HEADS):
        lo = hd * MEM_HEAD_DIM
        k_ref[0, :, lo:lo + MEM_HEAD_DIM] = (_rms(kv[:, lo:lo + MEM_HEAD_DIM], MEM_HEAD_DIM) * gk).astype(BF16)
    v_ref[0] = kv[:, MEM_WIDTH:].astype(BF16)


def _mem_kv(mem, g, w, gk):
    batch, m, d = mem.shape
    const = lambda b: (0, 0)
    shape = jax.ShapeDtypeStruct((batch, m, MEM_WIDTH), BF16)
    return pl.pallas_call(
        _mem_kv_kernel,
        out_shape=(shape, shape),
        grid=(batch,),
        in_specs=[pl.BlockSpec((1, m, d), lambda b: (b, 0, 0)),
                  pl.BlockSpec(g.shape, const),
                  pl.BlockSpec(w.shape, const),
                  pl.BlockSpec(gk.shape, const)],
        out_specs=(pl.BlockSpec((1, m, MEM_WIDTH), lambda b: (b, 0, 0)),
                   pl.BlockSpec((1, m, MEM_WIDTH), lambda b: (b, 0, 0))),
        compiler_params=_params("arbitrary"),
        name="mem_kv",
    )(mem, g, w, gk)


def _mem_attn_kernel(h_ref, w_ref, k_ref, v_ref, gq_ref, o_ref):
    p = _dot_nt(h_ref[...], w_ref[...])
    gq = gq_ref[...] * (MEM_HEAD_DIM ** -0.5)
    for hd in range(MEM_HEADS):
        lo = hd * MEM_HEAD_DIM
        q = (_rms(p[:, lo:lo + MEM_HEAD_DIM], MEM_HEAD_DIM) * gq).astype(BF16)
        z = p[:, MEM_WIDTH + lo:MEM_WIDTH + lo + MEM_HEAD_DIM]
        s = _dot_nt(q, k_ref[0, :, lo:lo + MEM_HEAD_DIM])
        e = jnp.exp(s - jnp.max(s, axis=-1, keepdims=True))
        y = _dot(e.astype(BF16), v_ref[0, :, lo:lo + MEM_HEAD_DIM])
        y = y * (1.0 / jnp.sum(e, axis=-1, keepdims=True))
        o_ref[:, lo:lo + MEM_HEAD_DIM] = (y * (z * _sigmoid(z))).astype(o_ref.dtype)


def _mem_attention(h, w, col0, k_mem, v_mem, gq, batch, seq, tm):
    t, d = h.shape
    ns = seq // tm
    m = k_mem.shape[1]
    const = lambda b, s: (0, 0)
    j0 = col0 // (2 * MEM_WIDTH)
    return pl.pallas_call(
        _mem_attn_kernel,
        out_shape=jax.ShapeDtypeStruct((t, MEM_WIDTH), BF16),
        grid=(batch, ns),
        in_specs=[pl.BlockSpec((tm, d), lambda b, s: (b * ns + s, 0)),
                  pl.BlockSpec((2 * MEM_WIDTH, d), lambda b, s: (j0, 0)),
                  pl.BlockSpec((1, m, MEM_WIDTH), lambda b, s: (b, 0, 0)),
                  pl.BlockSpec((1, m, MEM_WIDTH), lambda b, s: (b, 0, 0)),
                  pl.BlockSpec(gq.shape, const)],
        out_specs=pl.BlockSpec((tm, MEM_WIDTH), lambda b, s: (b * ns + s, 0)),
        compiler_params=_params("arbitrary", "arbitrary"),
        name="mem_attention",
    )(h, w, k_mem, v_mem, gq)


def _gate_kernel(h_ref, w_ref, o_ref, *, silu):
    p = _dot_nt(h_ref[...], w_ref[...])
    s = _sigmoid(p)
    o_ref[...] = ((p * s) if silu else s).astype(o_ref.dtype)


def _gate_proj(h, w, col0, n, silu, tm, tn):
    t, d = h.shape
    j0 = col0 // tn
    return pl.pallas_call(
        functools.partial(_gate_kernel, silu=silu),
        out_shape=jax.ShapeDtypeStruct((t, n), BF16),
        grid=(t // tm, n // tn),
        in_specs=[pl.BlockSpec((tm, d), lambda i, j: (i, 0)),
                  pl.BlockSpec((tn, d), lambda i, j: (j0 + j, 0))],
        out_specs=pl.BlockSpec((tm, tn), lambda i, j: (i, j)),
        compiler_params=_params("arbitrary", "arbitrary"),
        name="silu_gate_proj" if silu else "sigmoid_gate_proj",
    )(h, w)


def _merge_out_kernel(ac_ref, am_ref, ae_ref, gc_ref, gm_ref, ge_ref, x_ref, wc_ref, wm_ref, we_ref,
                      wo_ref, o_ref):
    merged = (gc_ref[...].astype(F32) * _dot(ac_ref[...], wc_ref[...])
              + gm_ref[...].astype(F32) * _dot(am_ref[...], wm_ref[...])
              + ge_ref[...].astype(F32) * _dot(ae_ref[...], we_ref[...]))
    o_ref[...] = x_ref[...] + _dot(merged.astype(BF16), wo_ref[...])


def _merge_out(a_conv, a_mla, a_mem, gates, x2d, w_conv, w_mla, w_mem, w_o, tm):
    t, d = x2d.shape
    row = lambda a: pl.BlockSpec((tm, a.shape[1]), lambda i: (i, 0))
    gate = lambda k: pl.BlockSpec((tm, d), lambda i: (i, k))
    wgt = lambda w: pl.BlockSpec(w.shape, lambda i: (0, 0), pipeline_mode=pl.Buffered(1))
    return pl.pallas_call(
        _merge_out_kernel,
        out_shape=jax.ShapeDtypeStruct((t, d), F32),
        grid=(t // tm,),
        in_specs=[row(a_conv), row(a_mla), row(a_mem), gate(0), gate(1), gate(2), row(x2d),
                  wgt(w_conv), wgt(w_mla), wgt(w_mem), wgt(w_o)],
        out_specs=pl.BlockSpec((tm, d), lambda i: (i, 0)),
        compiler_params=pltpu.CompilerParams(dimension_semantics=("arbitrary",),
                                             vmem_limit_bytes=MERGE_VMEM_LIMIT_BYTES),
        name="merge_out",
    )(a_conv, a_mla, a_mem, gates, gates, gates, x2d, w_conv, w_mla, w_mem, w_o)


def _layer(x, cost, sint, mem, norm_g, w_in, conv_w, w_conv_out, mla_q_norm_g, w_uq, mla_kv_norm_g,
           w_ukv, mla_qn_nope_g, mla_qn_rope_g, mla_kn_nope_g, mla_kn_rope_g, w_mla_out, mem_norm_g,
           w_mem_kv, mem_qn_g, mem_kn_g, w_mem_out, w_o):
    batch, seq, d = x.shape
    t = batch * seq
    x2d = x.reshape(t, d)

    o_rest = 4 * CONV_WIDTH + Q_LORA + KV_LORA + QK_ROPE
    w_front = w_in.T.astype(BF16)
    w_rest = w_front[o_rest:]
    w_uqt = w_uq.astype(BF16).T
    w_ukv_h = w_ukv.astype(BF16).reshape(KV_LORA, MLA_HEADS, QK_NOPE + V_HEAD)
    w_kn = w_ukv_h[:, :, :QK_NOPE].reshape(KV_LORA, MLA_HEADS * QK_NOPE)
    w_vt = w_ukv_h[:, :, QK_NOPE:].reshape(KV_LORA, MLA_WIDTH).T
    qscale = (QK_NOPE + QK_ROPE) ** -0.5 * LOG2_E

    conv_act, h = _norm_conv_branch(x2d, norm_g[None, :], w_front, conv_w, seq, tm=1024)
    qt, k, vt = _mla_prep(h, w_front, w_uqt, w_kn, w_vt, mla_q_norm_g[None, :],
                          mla_kv_norm_g[None, :], (mla_qn_nope_g * qscale)[:, None],
                          (mla_qn_rope_g * qscale)[:, None], mla_kn_nope_g[None, :],
                          mla_kn_rope_g[:, None], cost, sint, batch, seq, tm=256, tk=ATTN_TILE)
    sz_mla = _gate_proj(h, w_rest, 0, MLA_WIDTH, True, tm=1024, tn=2048)
    gates = _gate_proj(h, w_rest, MLA_WIDTH + 2 * MEM_WIDTH, 3 * d, False, tm=1024, tn=2048)
    mla_act = _mla_attention(qt, k, vt, sz_mla.reshape(batch, seq, MLA_WIDTH), tq=ATTN_TILE, heads=2)
    k_mem, v_mem = _mem_kv(mem, mem_norm_g[None, :], w_mem_kv.astype(BF16), mem_kn_g[None, :])
    mem_act = _mem_attention(h, w_rest, MLA_WIDTH, k_mem, v_mem, mem_qn_g[None, :], batch, seq, tm=512)
    out = _merge_out(conv_act, mla_act.reshape(t, MLA_WIDTH), mem_act, gates, x2d, w_conv_out.astype(BF16),
                     w_mla_out.astype(BF16), w_mem_out.astype(BF16), w_o.astype(BF16), tm=256)
    return out.reshape(batch, seq, d)


def kernel(x, positions, mem, norm_g, w_in, conv_w, w_conv_out, mla_q_norm_g, w_uq, mla_kv_norm_g, w_ukv, mla_qn_nope_g, mla_qn_rope_g, mla_kn_nope_g, mla_kn_rope_g, w_mla_out, mem_norm_g, w_mem_kv, mem_qn_g, mem_kn_g, w_mem_out, w_o):
    batch, seq, _ = x.shape
    half = QK_ROPE // 2
    inv_freq = jnp.power(ROPE_THETA, -jnp.arange(half, dtype=F32) / half)
    ang = positions.astype(F32)[..., None] * inv_freq
    cost = jnp.cos(ang).reshape(batch * seq, half).T
    sint = jnp.sin(ang).reshape(batch * seq, half).T
    for l in range(norm_g.shape[0]):
        x = _layer(x, cost, sint, mem, norm_g[l], w_in[l], conv_w[l], w_conv_out[l], mla_q_norm_g[l],
                   w_uq[l], mla_kv_norm_g[l], w_ukv[l], mla_qn_nope_g[l], mla_qn_rope_g[l],
                   mla_kn_nope_g[l], mla_kn_rope_g[l], w_mla_out[l], mem_norm_g[l], w_mem_kv[l],
                   mem_qn_g[l], mem_kn_g[l], w_mem_out[l], w_o[l])
    return x
```

```python
import functools

import jax
import jax.numpy as jnp
from jax import lax
from jax.experimental import pallas as pl
from jax.experimental.pallas import tpu as pltpu

F32 = jnp.float32
BF16 = jnp.bfloat16

EPS = 1e-6
CHUNK = 64
CONV_WIDTH = 1024
CONV_K = 3
CONV_GROUP = 256
MLA_HEADS = 16
Q_LORA = 512
KV_LORA = 512
QK_NOPE = 128
QK_ROPE = 64
V_HEAD = 128
MLA_WIDTH = MLA_HEADS * V_HEAD
HEAD_PAD = 256
ROPE_THETA = 10000.0
MEM_HEADS = 4
MEM_HEAD_DIM = 256
MEM_WIDTH = MEM_HEADS * MEM_HEAD_DIM
NEG_BIG = -1e30
LOG2_E = 1.4426950408889634
ATTN_TILE = 512
ATTN_GROUP = 4

VMEM_LIMIT_BYTES = 52 * 1024 * 1024
MERGE_VMEM_LIMIT_BYTES = 50 * 1024 * 1024


def _params(*sem):
    return pltpu.CompilerParams(dimension_semantics=sem, vmem_limit_bytes=VMEM_LIMIT_BYTES)


def _rms(x, width):
    ms = jnp.sum(x * x, axis=-1, keepdims=True) * (1.0 / width)
    return x * lax.rsqrt(ms + EPS)


def _sigmoid(x):
    return 0.5 * jnp.tanh(0.5 * x) + 0.5


def _dot(a, b):
    return jnp.dot(a, b, preferred_element_type=F32)


def _dot_nt(a, b):
    return lax.dot_general(a, b, (((1,), (1,)), ((), ())), preferred_element_type=F32)


def _conv_kernel(x_ref, ng_ref, wc_ref, wb_ref, wu_ref, wz_ref, cw_ref, o_ref, h_ref, hs_ref,
                 carry_ref, buf_ref, *, tm, tiles_per_seq):
    i = pl.program_id(0)
    g = pl.program_id(1)

    @pl.when(g == 0)
    def _():
        x = x_ref[...]
        hn = (_rms(x, x.shape[-1]) * ng_ref[...]).astype(BF16)
        hs_ref[...] = hn
        h_ref[...] = hn

    h = hs_ref[...]
    b = _dot_nt(h, wb_ref[...])
    z = _dot_nt(h, wz_ref[...])
    cu = _dot_nt(h, wc_ref[...]) * _dot_nt(h, wu_ref[...])
    seq_start = (i % tiles_per_seq) == 0
    buf_ref[0:8, :] = jnp.where(seq_start, 0.0, carry_ref[g])
    buf_ref[8:8 + tm, :] = cu
    carry_ref[g] = cu[tm - 8:tm, :]
    cw = cw_ref[...]
    y = (buf_ref[6:6 + tm, :] * cw[0:1, :] + buf_ref[7:7 + tm, :] * cw[1:2, :] + cu * cw[2:3, :])
    o_ref[...] = (b * y * (z * _sigmoid(z))).astype(o_ref.dtype)


def _norm_conv_branch(x2d, norm_g, w_front, conv_w, seq, tm):
    t, d = x2d.shape
    n_groups = CONV_WIDTH // CONV_GROUP
    kern = functools.partial(_conv_kernel, tm=tm, tiles_per_seq=seq // tm)
    part = lambda k: pl.BlockSpec((CONV_GROUP, d), lambda i, g: (k * n_groups + g, 0))
    return pl.pallas_call(
        kern,
        out_shape=(jax.ShapeDtypeStruct((t, CONV_WIDTH), BF16), jax.ShapeDtypeStruct((t, d), BF16)),
        grid=(t // tm, n_groups),
        in_specs=[pl.BlockSpec((tm, d), lambda i, g: (i, 0)),
                  pl.BlockSpec((1, d), lambda i, g: (0, 0)),
                  part(0), part(1), part(2), part(3),
                  pl.BlockSpec((CONV_K, CONV_GROUP), lambda i, g: (0, g))],
        out_specs=(pl.BlockSpec((tm, CONV_GROUP), lambda i, g: (i, g)),
                   pl.BlockSpec((tm, d), lambda i, g: (i, 0))),
        scratch_shapes=[pltpu.VMEM((tm, d), BF16),
                        pltpu.VMEM((n_groups, 8, CONV_GROUP), F32),
                        pltpu.VMEM((tm + 8, CONV_GROUP), F32)],
        compiler_params=_params("arbitrary", "arbitrary"),
        name="norm_conv_branch",
    )(x2d, norm_g, w_front, w_front, w_front, w_front, conv_w)


def _rope_t(x, g, cost, sint):
    half = QK_ROPE // 2
    x = x * lax.rsqrt(jnp.sum(x * x, axis=0, keepdims=True) * (1.0 / QK_ROPE) + EPS) * g
    x1 = x[0:half, :]
    x2 = x[half:QK_ROPE, :]
    return x1 * cost - x2 * sint, x2 * cost + x1 * sint


def _mla_prep_kernel(h_ref, win_ref, wkrt_ref, wuqt_ref, wkn_ref, wvt_ref, gq_ref, gkv_ref, gqn_ref,
                     gqr_ref, gkn_ref, gkr_ref, cost_ref, sint_ref, qt_ref, k_ref, vt_ref):
    tm = h_ref.shape[0]
    h = h_ref[...]
    p = _dot_nt(h, win_ref[...])
    cqn = (_rms(p[:, 0:Q_LORA], Q_LORA) * gq_ref[...]).astype(BF16)
    ckvn = (_rms(p[:, Q_LORA:Q_LORA + KV_LORA], KV_LORA) * gkv_ref[...]).astype(BF16)
    qt = _dot_nt(wuqt_ref[...], cqn)
    kn = _dot(ckvn, wkn_ref[...])
    vt = _dot_nt(wvt_ref[...], ckvn)
    krt = _dot_nt(wkrt_ref[...], h)

    cost = cost_ref[...]
    sint = sint_ref[...]
    kr1, kr2 = _rope_t(krt, jnp.broadcast_to(gkr_ref[...], (QK_ROPE, tm)), cost, sint)
    zeros_f = jnp.zeros((HEAD_PAD - QK_NOPE - QK_ROPE, tm), F32)
    k_rope = jnp.concatenate([kr1, kr2, zeros_f], axis=0).T.astype(BF16)

    gqn = jnp.broadcast_to(gqn_ref[...], (QK_NOPE, tm))
    gqr = jnp.broadcast_to(gqr_ref[...], (QK_ROPE, tm))
    gkn = gkn_ref[...]
    zeros = zeros_f.astype(BF16)
    for hd in range(MLA_HEADS):
        lo = hd * (QK_NOPE + QK_ROPE)
        qn = qt[lo:lo + QK_NOPE, :]
        qr = qt[lo + QK_NOPE:lo + QK_NOPE + QK_ROPE, :]
        qn = qn * lax.rsqrt(jnp.sum(qn * qn, axis=0, keepdims=True) * (1.0 / QK_NOPE) + EPS) * gqn
        r1, r2 = _rope_t(qr, gqr, cost, sint)
        half = QK_ROPE // 2
        qt_ref[0, hd, 0:QK_NOPE, :] = qn.astype(BF16)
        qt_ref[0, hd, QK_NOPE:QK_NOPE + half, :] = r1.astype(BF16)
        qt_ref[0, hd, QK_NOPE + half:QK_NOPE + QK_ROPE, :] = r2.astype(BF16)
        qt_ref[0, hd, QK_NOPE + QK_ROPE:HEAD_PAD, :] = zeros
        kh = kn[:, hd * QK_NOPE:(hd + 1) * QK_NOPE]
        k_ref[0, hd, :, 0:QK_NOPE] = (_rms(kh, QK_NOPE) * gkn).astype(BF16)
        k_ref[0, hd, :, QK_NOPE:HEAD_PAD] = k_rope
        vt_ref[0, hd, 0, :, :] = vt[hd * V_HEAD:(hd + 1) * V_HEAD, :].astype(BF16)


def _mla_prep(h, w_front, w_uqt, w_kn, w_vt, gq, gkv, gqn, gqr, gkn, gkr, cost, sint,
              batch, seq, tm, tk):
    t, d = h.shape
    ns = seq // tm
    per_chunk = tk // tm
    lora = Q_LORA + KV_LORA
    const = lambda b, s: (0, 0)
    row = lambda b, s: (b * ns + s, 0)
    col = lambda b, s: (0, b * ns + s)
    qt_shape = jax.ShapeDtypeStruct((batch, MLA_HEADS, HEAD_PAD, seq), BF16)
    k_shape = jax.ShapeDtypeStruct((batch, MLA_HEADS, seq, HEAD_PAD), BF16)
    vt_shape = jax.ShapeDtypeStruct((batch, MLA_HEADS, seq // tk, V_HEAD, tk), BF16)
    full = lambda a: pl.BlockSpec(a.shape, const)
    return pl.pallas_call(
        _mla_prep_kernel,
        out_shape=(qt_shape, k_shape, vt_shape),
        grid=(batch, ns),
        in_specs=[pl.BlockSpec((tm, d), row),
                  pl.BlockSpec((lora, d), lambda b, s: (4 * CONV_WIDTH // lora, 0)),
                  pl.BlockSpec((QK_ROPE, d), lambda b, s: ((4 * CONV_WIDTH + lora) // QK_ROPE, 0)),
                  full(w_uqt), full(w_kn), full(w_vt),
                  full(gq), full(gkv), full(gqn), full(gqr), full(gkn), full(gkr),
                  pl.BlockSpec((QK_ROPE // 2, tm), col), pl.BlockSpec((QK_ROPE // 2, tm), col)],
        out_specs=(pl.BlockSpec((1, MLA_HEADS, HEAD_PAD, tm), lambda b, s: (b, 0, 0, s)),
                   pl.BlockSpec((1, MLA_HEADS, tm, HEAD_PAD), lambda b, s: (b, 0, s, 0)),
                   pl.BlockSpec((1, MLA_HEADS, 1, V_HEAD, tm),
                                lambda b, s: (b, 0, s // per_chunk, 0, s % per_chunk))),
        compiler_params=_params("arbitrary", "arbitrary"),
        name="mla_prep",
    )(h, w_front, w_front, w_uqt, w_kn, w_vt, gq, gkv, gqn, gqr, gkn, gkr, cost, sint)


def _attn_kernel(qt_ref, qtn_ref, k_ref, vt_ref, sz_ref, o_ref, acc_ref, m_ref, l_ref, s0_ref, s1_ref,
                 *, tq, heads):
    j = pl.program_id(2)
    s_bufs = (s0_ref, s1_ref)

    def scores(hh, ki, q, dst):
        start = pl.multiple_of(ki * tq, tq)
        s = _dot(k_ref[0, hh, pl.ds(start, tq), :], q)
        dst[hh, 0:tq, :] = s
        dst[hh, tq:tq + 1, :] = jnp.max(s, axis=0, keepdims=True)

    def update(hh, ki, src, diagonal):
        s = src[hh, 0:tq, :]
        if diagonal:
            kc = lax.broadcasted_iota(jnp.int32, s.shape, 0) // CHUNK
            qc = lax.broadcasted_iota(jnp.int32, s.shape, 1) // CHUNK
            s = jnp.where(kc <= qc, s, NEG_BIG)
            m_tile = jnp.max(s, axis=0, keepdims=True)
        else:
            m_tile = src[hh, tq:tq + 1, :]
        m = m_ref[hh]
        m_new = jnp.maximum(m, m_tile)
        alpha = jnp.exp2(m - m_new)
        p = jnp.exp2(s - m_new)
        m_ref[hh] = m_new
        l_ref[hh] = alpha * l_ref[hh] + jnp.sum(p, axis=0, keepdims=True)
        acc_ref[hh] = alpha * acc_ref[hh] + _dot(vt_ref[0, hh, ki], p.astype(BF16))

    @pl.when(j == 0)
    def _():
        for hh in range(heads):
            scores(hh, 0, qt_ref[0, hh, :, 0:tq], s0_ref)

    for u in range(ATTN_GROUP):
        q_tile = ATTN_GROUP * j + u
        first = ((u + 1) // 2) % 2
        bufs = (s_bufs[first], s_bufs[1 - first])

        def q_cur(hh, u=u):
            return qt_ref[0, hh, :, u * tq:(u + 1) * tq]

        def q_next(hh, u=u):
            if u + 1 < ATTN_GROUP:
                return qt_ref[0, hh, :, (u + 1) * tq:(u + 2) * tq]
            return qtn_ref[0, hh]

        acc_ref[...] = jnp.zeros_like(acc_ref)
        l_ref[...] = jnp.zeros_like(l_ref)
        m_ref[...] = jnp.full(m_ref.shape, NEG_BIG, F32)

        def step(ki, parity, bufs=bufs, q_cur=q_cur):
            for hh in range(heads):
                scores(hh, ki + 1, q_cur(hh), bufs[1 - parity])
                update(hh, ki, bufs[parity], False)

        def group(jj, carry, step=step):
            for r in range(ATTN_GROUP):
                step(ATTN_GROUP * jj + r, r % 2)
            return carry

        lax.fori_loop(0, j, group, 0)
        for r in range(u):
            step(ATTN_GROUP * j + r, r % 2)
        for hh in range(heads):
            scores(hh, 0, q_next(hh), bufs[1 - u % 2])
            update(hh, q_tile, bufs[u % 2], True)
            y = (acc_ref[hh] * (1.0 / l_ref[hh])).T
            sz = sz_ref[0, u * tq:(u + 1) * tq, hh * V_HEAD:(hh + 1) * V_HEAD].astype(F32)
            o_ref[0, u * tq:(u + 1) * tq, hh * V_HEAD:(hh + 1) * V_HEAD] = (y * sz).astype(o_ref.dtype)


def _mla_attention(qt, k, vt, sz, tq, heads):
    batch, n_heads, seq, _ = k.shape
    assert ATTN_GROUP % 4 == 0
    kern = functools.partial(_attn_kernel, tq=tq, heads=heads)
    tg = ATTN_GROUP * tq
    last = seq // tq - 1
    return pl.pallas_call(
        kern,
        out_shape=jax.ShapeDtypeStruct((batch, seq, MLA_WIDTH), BF16),
        grid=(batch, n_heads // heads, seq // tg),
        in_specs=[pl.BlockSpec((1, heads, HEAD_PAD, tg), lambda b, h, i: (b, h, 0, i)),
                  pl.BlockSpec((1, heads, HEAD_PAD, tq),
                               lambda b, h, i: (b, h, 0, jnp.minimum(ATTN_GROUP * (i + 1), last))),
                  pl.BlockSpec((1, heads, seq, HEAD_PAD), lambda b, h, i: (b, h, 0, 0)),
                  pl.BlockSpec((1, heads, seq // tq, V_HEAD, tq), lambda b, h, i: (b, h, 0, 0, 0)),
                  pl.BlockSpec((1, tg, heads * V_HEAD), lambda b, h, i: (b, i, h))],
        out_specs=pl.BlockSpec((1, tg, heads * V_HEAD), lambda b, h, i: (b, i, h)),
        scratch_shapes=[pltpu.VMEM((heads, V_HEAD, tq), F32),
                        pltpu.VMEM((heads, 1, tq), F32),
                        pltpu.VMEM((heads, 1, tq), F32),
                        pltpu.VMEM((heads, tq + 8, tq), F32),
                        pltpu.VMEM((heads, tq + 8, tq), F32)],
        compiler_params=_params("arbitrary", "arbitrary", "arbitrary"),
        name="mla_attention",
    )(qt, qt, k, vt, sz)


def _mem_kv_kernel(mem_ref, g_ref, w_ref, gk_ref, k_ref, v_ref):
    m = mem_ref[0]
    kv = _dot((_rms(m, m.shape[-1]) * g_ref[...]).astype(BF16), w_ref[...])
    gk = gk_ref[...]
    for hd in range(MEM_HEADS):
        lo = hd * MEM_HEAD_DIM
        k_ref[0, :, lo:lo + MEM_HEAD_DIM] = (_rms(kv[:, lo:lo + MEM_HEAD_DIM], MEM_HEAD_DIM) * gk).astype(BF16)
    v_ref[0] = kv[:, MEM_WIDTH:].astype(BF16)


def _mem_kv(mem, g, w, gk):
    batch, m, d = mem.shape
    const = lambda b: (0, 0)
    shape = jax.ShapeDtypeStruct((batch, m, MEM_WIDTH), BF16)
    return pl.pallas_call(
        _mem_kv_kernel,
        out_shape=(shape, shape),
        grid=(batch,),
        in_specs=[pl.BlockSpec((1, m, d), lambda b: (b, 0, 0)),
                  pl.BlockSpec(g.shape, const),
                  pl.BlockSpec(w.shape, const),
                  pl.BlockSpec(gk.shape, const)],
        out_specs=(pl.BlockSpec((1, m, MEM_WIDTH), lambda b: (b, 0, 0)),
                   pl.BlockSpec((1, m, MEM_WIDTH), lambda b: (b, 0, 0))),
        compiler_params=_params("arbitrary"),
        name="mem_kv",
    )(mem, g, w, gk)


def _mem_attn_kernel(h_ref, w_ref, k_ref, v_ref, gq_ref, o_ref):
    p = _dot_nt(h_ref[...], w_ref[...])
    gq = gq_ref[...] * (MEM_HEAD_DIM ** -0.5)
    for hd in range(MEM_HEADS):
        lo = hd * MEM_HEAD_DIM
        q = (_rms(p[:, lo:lo + MEM_HEAD_DIM], MEM_HEAD_DIM) * gq).astype(BF16)
        z = p[:, MEM_WIDTH + lo:MEM_WIDTH + lo + MEM_HEAD_DIM]
        s = _dot_nt(q, k_ref[0, :, lo:lo + MEM_HEAD_DIM])
        e = jnp.exp(s - jnp.max(s, axis=-1, keepdims=True))
        y = _dot(e.astype(BF16), v_ref[0, :, lo:lo + MEM_HEAD_DIM])
        y = y * (1.0 / jnp.sum(e, axis=-1, keepdims=True))
        o_ref[:, lo:lo + MEM_HEAD_DIM] = (y * (z * _sigmoid(z))).astype(o_ref.dtype)


def _mem_attention(h, w, col0, k_mem, v_mem, gq, batch, seq, tm):
    t, d = h.shape
    ns = seq // tm
    m = k_mem.shape[1]
    const = lambda b, s: (0, 0)
    return pl.pallas_call(
        _mem_attn_kernel,
        out_shape=jax.ShapeDtypeStruct((t, MEM_WIDTH), BF16),
        grid=(batch, ns),
        in_specs=[pl.BlockSpec((tm, d), lambda b, s: (b * ns + s, 0)),
                  pl.BlockSpec((pl.Element(2 * MEM_WIDTH), pl.Element(d)), lambda b, s: (col0, 0)),
                  pl.BlockSpec((1, m, MEM_WIDTH), lambda b, s: (b, 0, 0)),
                  pl.BlockSpec((1, m, MEM_WIDTH), lambda b, s: (b, 0, 0)),
                  pl.BlockSpec(gq.shape, const)],
        out_specs=pl.BlockSpec((tm, MEM_WIDTH), lambda b, s: (b * ns + s, 0)),
        compiler_params=_params("arbitrary", "arbitrary"),
        name="mem_attention",
    )(h, w, k_mem, v_mem, gq)


def _gate_kernel(h_ref, w_ref, o_ref, *, silu):
    p = _dot_nt(h_ref[...], w_ref[...])
    s = _sigmoid(p)
    o_ref[...] = ((p * s) if silu else s).astype(o_ref.dtype)


def _gate_proj(h, w, col0, n, silu, tm, tn):
    t, d = h.shape
    return pl.pallas_call(
        functools.partial(_gate_kernel, silu=silu),
        out_shape=jax.ShapeDtypeStruct((t, n), BF16),
        grid=(t // tm, n // tn),
        in_specs=[pl.BlockSpec((tm, d), lambda i, j: (i, 0)),
                  pl.BlockSpec((pl.Element(tn), pl.Element(d)),
                               lambda i, j: (pl.multiple_of(col0 + j * tn, 16), 0))],
        out_specs=pl.BlockSpec((tm, tn), lambda i, j: (i, j)),
        compiler_params=_params("arbitrary", "arbitrary"),
        name="silu_gate_proj" if silu else "sigmoid_gate_proj",
    )(h, w)


def _merge_out_kernel(ac_ref, am_ref, ae_ref, gc_ref, gm_ref, ge_ref, x_ref, wc_ref, wm_ref, we_ref,
                      wo_ref, o_ref):
    merged = (gc_ref[...].astype(F32) * _dot(ac_ref[...], wc_ref[...])
              + gm_ref[...].astype(F32) * _dot(am_ref[...], wm_ref[...])
              + ge_ref[...].astype(F32) * _dot(ae_ref[...], we_ref[...]))
    o_ref[...] = x_ref[...] + _dot(merged.astype(BF16), wo_ref[...])


def _merge_out(a_conv, a_mla, a_mem, gates, x2d, w_conv, w_mla, w_mem, w_o, tm):
    t, d = x2d.shape
    row = lambda a: pl.BlockSpec((tm, a.shape[1]), lambda i: (i, 0))
    gate = lambda k: pl.BlockSpec((tm, d), lambda i: (i, k))
    wgt = lambda w: pl.BlockSpec(w.shape, lambda i: (0, 0), pipeline_mode=pl.Buffered(1))
    return pl.pallas_call(
        _merge_out_kernel,
        out_shape=jax.ShapeDtypeStruct((t, d), F32),
        grid=(t // tm,),
        in_specs=[row(a_conv), row(a_mla), row(a_mem), gate(0), gate(1), gate(2), row(x2d),
                  wgt(w_conv), wgt(w_mla), wgt(w_mem), wgt(w_o)],
        out_specs=pl.BlockSpec((tm, d), lambda i: (i, 0)),
        compiler_params=pltpu.CompilerParams(dimension_semantics=("arbitrary",),
                                             vmem_limit_bytes=MERGE_VMEM_LIMIT_BYTES),
        name="merge_out",
    )(a_conv, a_mla, a_mem, gates, gates, gates, x2d, w_conv, w_mla, w_mem, w_o)


def _layer(x, cost, sint, mem, norm_g, w_in, conv_w, w_conv_out, mla_q_norm_g, w_uq, mla_kv_norm_g,
           w_ukv, mla_qn_nope_g, mla_qn_rope_g, mla_kn_nope_g, mla_kn_rope_g, w_mla_out, mem_norm_g,
           w_mem_kv, mem_qn_g, mem_kn_g, w_mem_out, w_o):
    batch, seq, d = x.shape
    t = batch * seq
    x2d = x.reshape(t, d)

    o_rest = 4 * CONV_WIDTH + Q_LORA + KV_LORA + QK_ROPE
    w_front = w_in.T.astype(BF16)
    w_uqt = w_uq.astype(BF16).T
    w_ukv_h = w_ukv.astype(BF16).reshape(KV_LORA, MLA_HEADS, QK_NOPE + V_HEAD)
    w_kn = w_ukv_h[:, :, :QK_NOPE].reshape(KV_LORA, MLA_HEADS * QK_NOPE)
    w_vt = w_ukv_h[:, :, QK_NOPE:].reshape(KV_LORA, MLA_WIDTH).T
    qscale = (QK_NOPE + QK_ROPE) ** -0.5 * LOG2_E

    conv_act, h = _norm_conv_branch(x2d, norm_g[None, :], w_front, conv_w, seq, tm=1024)
    qt, k, vt = _mla_prep(h, w_front, w_uqt, w_kn, w_vt, mla_q_norm_g[None, :],
                          mla_kv_norm_g[None, :], (mla_qn_nope_g * qscale)[:, None],
                          (mla_qn_rope_g * qscale)[:, None], mla_kn_nope_g[None, :],
                          mla_kn_rope_g[:, None], cost, sint, batch, seq, tm=256, tk=ATTN_TILE)
    sz_mla = _gate_proj(h, w_front, o_rest, MLA_WIDTH, True, tm=1024, tn=2048)
    gates = _gate_proj(h, w_front, o_rest + MLA_WIDTH + 2 * MEM_WIDTH, 3 * d, False, tm=1024, tn=2048)
    mla_act = _mla_attention(qt, k, vt, sz_mla.reshape(batch, seq, MLA_WIDTH), tq=ATTN_TILE, heads=2)
    k_mem, v_mem = _mem_kv(mem, mem_norm_g[None, :], w_mem_kv.astype(BF16), mem_kn_g[None, :])
    mem_act = _mem_attention(h, w_front, o_rest + MLA_WIDTH, k_mem, v_mem, mem_qn_g[None, :], batch, seq,
                             tm=512)
    out = _merge_out(conv_act, mla_act.reshape(t, MLA_WIDTH), mem_act, gates, x2d, w_conv_out.astype(BF16),
                     w_mla_out.astype(BF16), w_mem_out.astype(BF16), w_o.astype(BF16), tm=256)
    return out.reshape(batch, seq, d)


def kernel(x, positions, mem, norm_g, w_in, conv_w, w_conv_out, mla_q_norm_g, w_uq, mla_kv_norm_g, w_ukv, mla_qn_nope_g, mla_qn_rope_g, mla_kn_nope_g, mla_kn_rope_g, w_mla_out, mem_norm_g, w_mem_kv, mem_qn_g, mem_kn_g, w_mem_out, w_o):
    batch, seq, _ = x.shape
    half = QK_ROPE // 2
    inv_freq = jnp.power(ROPE_THETA, -jnp.arange(half, dtype=F32) / half)
    ang = positions.astype(F32)[..., None] * inv_freq
    cost = jnp.cos(ang).reshape(batch * seq, half).T
    sint = jnp.sin(ang).reshape(batch * seq, half).T
    for l in range(norm_g.shape[0]):
        x = _layer(x, cost, sint, mem, norm_g[l], w_in[l], conv_w[l], w_conv_out[l], mla_q_norm_g[l],
                   w_uq[l], mla_kv_norm_g[l], w_ukv[l], mla_qn_nope_g[l], mla_qn_rope_g[l],
                   mla_kn_nope_g[l], mla_kn_rope_g[l], w_mla_out[l], mem_norm_g[l], w_mem_kv[l],
                   mem_qn_g[l], mem_kn_g[l], w_mem_out[l], w_o[l])
    return x
```

```python
import functools

import jax
import jax.numpy as jnp
from jax import lax
from jax.experimental import pallas as pl
from jax.experimental.pallas import tpu as pltpu

F32 = jnp.float32
BF16 = jnp.bfloat16

EPS = 1e-6
CHUNK = 64
CONV_WIDTH = 1024
CONV_K = 3
CONV_GROUP = 256
MLA_HEADS = 16
Q_LORA = 512
KV_LORA = 512
QK_NOPE = 128
QK_ROPE = 64
V_HEAD = 128
MLA_WIDTH = MLA_HEADS * V_HEAD
HEAD_PAD = 256
ROPE_THETA = 10000.0
MEM_HEADS = 4
MEM_HEAD_DIM = 256
MEM_WIDTH = MEM_HEADS * MEM_HEAD_DIM
NEG_BIG = -1e30
LOG2_E = 1.4426950408889634
ATTN_TILE = 512
ATTN_GROUP = 4

SUBLANES = 8
BF16_SUBLANES = 16
VMEM_LIMIT_BYTES = 52 * 1024 * 1024
MERGE_VMEM_LIMIT_BYTES = 50 * 1024 * 1024


def _params(*sem):
    return pltpu.CompilerParams(dimension_semantics=sem, vmem_limit_bytes=VMEM_LIMIT_BYTES)


def _rms(x, width):
    ms = jnp.sum(x * x, axis=-1, keepdims=True) * (1.0 / width)
    return x * lax.rsqrt(ms + EPS)


def _sigmoid(x):
    return 0.5 * jnp.tanh(0.5 * x) + 0.5


def _dot(a, b):
    return jnp.dot(a, b, preferred_element_type=F32)


def _dot_nt(a, b):
    return lax.dot_general(a, b, (((1,), (1,)), ((), ())), preferred_element_type=F32)


def _conv_kernel(x_ref, ng_ref, wc_ref, wb_ref, wu_ref, wz_ref, cw_ref, o_ref, h_ref,
                 carry_ref, buf_ref, *, tm, tiles_per_seq):
    i = pl.program_id(0)
    g = pl.program_id(1)

    @pl.when(g == 0)
    def _():
        x = x_ref[...]
        h_ref[...] = (_rms(x, x.shape[-1]) * ng_ref[...]).astype(BF16)

    h = h_ref[...]
    b = _dot_nt(h, wb_ref[...])
    z = _dot_nt(h, wz_ref[...])
    cu = _dot_nt(h, wc_ref[...]) * _dot_nt(h, wu_ref[...])
    seq_start = (i % tiles_per_seq) == 0
    buf_ref[0:SUBLANES, :] = jnp.where(seq_start, 0.0, carry_ref[g])
    buf_ref[SUBLANES:SUBLANES + tm, :] = cu
    carry_ref[g] = cu[tm - SUBLANES:tm, :]
    cw = cw_ref[...]
    y = cu * cw[CONV_K - 1:CONV_K, :]
    for tap in range(CONV_K - 1):
        lag = CONV_K - 1 - tap
        y = y + buf_ref[SUBLANES - lag:SUBLANES - lag + tm, :] * cw[tap:tap + 1, :]
    o_ref[...] = (b * y * (z * _sigmoid(z))).astype(o_ref.dtype)


def _norm_conv_branch(x2d, norm_g, w_front, conv_w, seq, tm):
    t, d = x2d.shape
    n_groups = CONV_WIDTH // CONV_GROUP
    kern = functools.partial(_conv_kernel, tm=tm, tiles_per_seq=seq // tm)
    part = lambda k: pl.BlockSpec((CONV_GROUP, d), lambda i, g: (k * n_groups + g, 0))
    return pl.pallas_call(
        kern,
        out_shape=(jax.ShapeDtypeStruct((t, CONV_WIDTH), BF16), jax.ShapeDtypeStruct((t, d), BF16)),
        grid=(t // tm, n_groups),
        in_specs=[pl.BlockSpec((tm, d), lambda i, g: (i, 0)),
                  pl.BlockSpec((1, d), lambda i, g: (0, 0)),
                  part(0), part(1), part(2), part(3),
                  pl.BlockSpec((CONV_K, CONV_GROUP), lambda i, g: (0, g))],
        out_specs=(pl.BlockSpec((tm, CONV_GROUP), lambda i, g: (i, g)),
                   pl.BlockSpec((tm, d), lambda i, g: (i, 0))),
        scratch_shapes=[pltpu.VMEM((n_groups, SUBLANES, CONV_GROUP), F32),
                        pltpu.VMEM((tm + SUBLANES, CONV_GROUP), F32)],
        compiler_params=_params("arbitrary", "arbitrary"),
        name="norm_conv_branch",
    )(x2d, norm_g, w_front, w_front, w_front, w_front, conv_w)


def _rope_t(x, g, cost, sint):
    half = QK_ROPE // 2
    x = x * lax.rsqrt(jnp.sum(x * x, axis=0, keepdims=True) * (1.0 / QK_ROPE) + EPS) * g
    x1 = x[0:half, :]
    x2 = x[half:QK_ROPE, :]
    return x1 * cost - x2 * sint, x2 * cost + x1 * sint


def _mla_prep_kernel(h_ref, win_ref, wkrt_ref, wuqt_ref, wkn_ref, wvt_ref, gq_ref, gkv_ref, gqn_ref,
                     gqr_ref, gkn_ref, gkr_ref, cost_ref, sint_ref, qt_ref, k_ref, vt_ref):
    tm = h_ref.shape[0]
    h = h_ref[...]
    p = _dot_nt(h, win_ref[...])
    cqn = (_rms(p[:, 0:Q_LORA], Q_LORA) * gq_ref[...]).astype(BF16)
    ckvn = (_rms(p[:, Q_LORA:Q_LORA + KV_LORA], KV_LORA) * gkv_ref[...]).astype(BF16)
    qt = _dot_nt(wuqt_ref[...], cqn)
    kn = _dot(ckvn, wkn_ref[...])
    vt = _dot_nt(wvt_ref[...], ckvn)
    krt = _dot_nt(wkrt_ref[...], h)

    cost = cost_ref[...]
    sint = sint_ref[...]
    kr1, kr2 = _rope_t(krt, jnp.broadcast_to(gkr_ref[...], (QK_ROPE, tm)), cost, sint)
    zeros_f = jnp.zeros((HEAD_PAD - QK_NOPE - QK_ROPE, tm), F32)
    k_rope = jnp.concatenate([kr1, kr2, zeros_f], axis=0).T.astype(BF16)

    gqn = jnp.broadcast_to(gqn_ref[...], (QK_NOPE, tm))
    gqr = jnp.broadcast_to(gqr_ref[...], (QK_ROPE, tm))
    gkn = gkn_ref[...]
    zeros = zeros_f.astype(BF16)
    for hd in range(MLA_HEADS):
        lo = hd * (QK_NOPE + QK_ROPE)
        qn = qt[lo:lo + QK_NOPE, :]
        qr = qt[lo + QK_NOPE:lo + QK_NOPE + QK_ROPE, :]
        qn = qn * lax.rsqrt(jnp.sum(qn * qn, axis=0, keepdims=True) * (1.0 / QK_NOPE) + EPS) * gqn
        r1, r2 = _rope_t(qr, gqr, cost, sint)
        half = QK_ROPE // 2
        qt_ref[0, hd, 0:QK_NOPE, :] = qn.astype(BF16)
        qt_ref[0, hd, QK_NOPE:QK_NOPE + half, :] = r1.astype(BF16)
        qt_ref[0, hd, QK_NOPE + half:QK_NOPE + QK_ROPE, :] = r2.astype(BF16)
        qt_ref[0, hd, QK_NOPE + QK_ROPE:HEAD_PAD, :] = zeros
        kh = kn[:, hd * QK_NOPE:(hd + 1) * QK_NOPE]
        k_ref[0, hd, :, 0:QK_NOPE] = (_rms(kh, QK_NOPE) * gkn).astype(BF16)
        k_ref[0, hd, :, QK_NOPE:HEAD_PAD] = k_rope
        vt_ref[0, hd, 0, :, :] = vt[hd * V_HEAD:(hd + 1) * V_HEAD, :].astype(BF16)


def _mla_prep(h, w_front, w_uqt, w_kn, w_vt, gq, gkv, gqn, gqr, gkn, gkr, cost, sint,
              batch, seq, tm, tk):
    t, d = h.shape
    ns = seq // tm
    per_chunk = tk // tm
    lora = Q_LORA + KV_LORA
    const = lambda b, s: (0, 0)
    row = lambda b, s: (b * ns + s, 0)
    col = lambda b, s: (0, b * ns + s)
    qt_shape = jax.ShapeDtypeStruct((batch, MLA_HEADS, HEAD_PAD, seq), BF16)
    k_shape = jax.ShapeDtypeStruct((batch, MLA_HEADS, seq, HEAD_PAD), BF16)
    vt_shape = jax.ShapeDtypeStruct((batch, MLA_HEADS, seq // tk, V_HEAD, tk), BF16)
    full = lambda a: pl.BlockSpec(a.shape, const)
    return pl.pallas_call(
        _mla_prep_kernel,
        out_shape=(qt_shape, k_shape, vt_shape),
        grid=(batch, ns),
        in_specs=[pl.BlockSpec((tm, d), row),
                  pl.BlockSpec((lora, d), lambda b, s: (4 * CONV_WIDTH // lora, 0)),
                  pl.BlockSpec((QK_ROPE, d), lambda b, s: ((4 * CONV_WIDTH + lora) // QK_ROPE, 0)),
                  full(w_uqt), full(w_kn), full(w_vt),
                  full(gq), full(gkv), full(gqn), full(gqr), full(gkn), full(gkr),
                  pl.BlockSpec((QK_ROPE // 2, tm), col), pl.BlockSpec((QK_ROPE // 2, tm), col)],
        out_specs=(pl.BlockSpec((1, MLA_HEADS, HEAD_PAD, tm), lambda b, s: (b, 0, 0, s)),
                   pl.BlockSpec((1, MLA_HEADS, tm, HEAD_PAD), lambda b, s: (b, 0, s, 0)),
                   pl.BlockSpec((1, MLA_HEADS, 1, V_HEAD, tm),
                                lambda b, s: (b, 0, s // per_chunk, 0, s % per_chunk))),
        compiler_params=_params("arbitrary", "arbitrary"),
        name="mla_prep",
    )(h, w_front, w_front, w_uqt, w_kn, w_vt, gq, gkv, gqn, gqr, gkn, gkr, cost, sint)


def _attn_kernel(qt_ref, qtn_ref, k_ref, vt_ref, sz_ref, o_ref, acc_ref, m_ref, l_ref, s0_ref, s1_ref,
                 *, tq, heads):
    j = pl.program_id(2)
    s_bufs = (s0_ref, s1_ref)

    def scores(hh, ki, q, dst):
        start = pl.multiple_of(ki * tq, tq)
        s = _dot(k_ref[0, hh, pl.ds(start, tq), :], q)
        dst[hh, 0:tq, :] = s
        dst[hh, tq:tq + 1, :] = jnp.max(s, axis=0, keepdims=True)

    def update(hh, ki, src, diagonal):
        s = src[hh, 0:tq, :]
        if diagonal:
            kc = lax.broadcasted_iota(jnp.int32, s.shape, 0) // CHUNK
            qc = lax.broadcasted_iota(jnp.int32, s.shape, 1) // CHUNK
            s = jnp.where(kc <= qc, s, NEG_BIG)
            m_tile = jnp.max(s, axis=0, keepdims=True)
        else:
            m_tile = src[hh, tq:tq + 1, :]
        m = m_ref[hh]
        m_new = jnp.maximum(m, m_tile)
        alpha = jnp.exp2(m - m_new)
        p = jnp.exp2(s - m_new)
        m_ref[hh] = m_new
        l_ref[hh] = alpha * l_ref[hh] + jnp.sum(p, axis=0, keepdims=True)
        acc_ref[hh] = alpha * acc_ref[hh] + _dot(vt_ref[0, hh, ki], p.astype(BF16))

    @pl.when(j == 0)
    def _():
        for hh in range(heads):
            scores(hh, 0, qt_ref[0, hh, :, 0:tq], s0_ref)

    for u in range(ATTN_GROUP):
        q_tile = ATTN_GROUP * j + u
        first = ((u + 1) // 2) % 2
        bufs = (s_bufs[first], s_bufs[1 - first])

        def q_cur(hh, u=u):
            return qt_ref[0, hh, :, u * tq:(u + 1) * tq]

        def q_next(hh, u=u):
            if u + 1 < ATTN_GROUP:
                return qt_ref[0, hh, :, (u + 1) * tq:(u + 2) * tq]
            return qtn_ref[0, hh]

        acc_ref[...] = jnp.zeros_like(acc_ref)
        l_ref[...] = jnp.zeros_like(l_ref)
        m_ref[...] = jnp.full(m_ref.shape, NEG_BIG, F32)

        def step(ki, parity, bufs=bufs, q_cur=q_cur):
            for hh in range(heads):
                scores(hh, ki + 1, q_cur(hh), bufs[1 - parity])
                update(hh, ki, bufs[parity], False)

        def group(jj, carry, step=step):
            for r in range(ATTN_GROUP):
                step(ATTN_GROUP * jj + r, r % 2)
            return carry

        lax.fori_loop(0, j, group, 0)
        for r in range(u):
            step(ATTN_GROUP * j + r, r % 2)
        for hh in range(heads):
            scores(hh, 0, q_next(hh), bufs[1 - u % 2])
            update(hh, q_tile, bufs[u % 2], True)
            y = (acc_ref[hh] * (1.0 / l_ref[hh])).T
            sz = sz_ref[0, u * tq:(u + 1) * tq, hh * V_HEAD:(hh + 1) * V_HEAD].astype(F32)
            o_ref[0, u * tq:(u + 1) * tq, hh * V_HEAD:(hh + 1) * V_HEAD] = (y * sz).astype(o_ref.dtype)


def _mla_attention(qt, k, vt, sz, tq, heads):
    batch, n_heads, seq, _ = k.shape
    assert ATTN_GROUP % 4 == 0
    kern = functools.partial(_attn_kernel, tq=tq, heads=heads)
    tg = ATTN_GROUP * tq
    last = seq // tq - 1
    return pl.pallas_call(
        kern,
        out_shape=jax.ShapeDtypeStruct((batch, seq, MLA_WIDTH), BF16),
        grid=(batch, n_heads // heads, seq // tg),
        in_specs=[pl.BlockSpec((1, heads, HEAD_PAD, tg), lambda b, h, i: (b, h, 0, i)),
                  pl.BlockSpec((1, heads, HEAD_PAD, tq),
                               lambda b, h, i: (b, h, 0, jnp.minimum(ATTN_GROUP * (i + 1), last))),
                  pl.BlockSpec((1, heads, seq, HEAD_PAD), lambda b, h, i: (b, h, 0, 0)),
                  pl.BlockSpec((1, heads, seq // tq, V_HEAD, tq), lambda b, h, i: (b, h, 0, 0, 0)),
                  pl.BlockSpec((1, tg, heads * V_HEAD), lambda b, h, i: (b, i, h))],
        out_specs=pl.BlockSpec((1, tg, heads * V_HEAD), lambda b, h, i: (b, i, h)),
        scratch_shapes=[pltpu.VMEM((heads, V_HEAD, tq), F32),
                        pltpu.VMEM((heads, 1, tq), F32),
                        pltpu.VMEM((heads, 1, tq), F32),
                        pltpu.VMEM((heads, tq + SUBLANES, tq), F32),
                        pltpu.VMEM((heads, tq + SUBLANES, tq), F32)],
        compiler_params=_params("arbitrary", "arbitrary", "arbitrary"),
        name="mla_attention",
    )(qt, qt, k, vt, sz)


def _mem_kv_kernel(mem_ref, g_ref, w_ref, gk_ref, k_ref, v_ref):
    m = mem_ref[0]
    kv = _dot((_rms(m, m.shape[-1]) * g_ref[...]).astype(BF16), w_ref[...])
    gk = gk_ref[...]
    for hd in range(MEM_HEADS):
        lo = hd * MEM_HEAD_DIM
        k_ref[0, :, lo:lo + MEM_HEAD_DIM] = (_rms(kv[:, lo:lo + MEM_HEAD_DIM], MEM_HEAD_DIM) * gk).astype(BF16)
    v_ref[0] = kv[:, MEM_WIDTH:].astype(BF16)


def _mem_kv(mem, g, w, gk):
    batch, m, d = mem.shape
    const = lambda b: (0, 0)
    shape = jax.ShapeDtypeStruct((batch, m, MEM_WIDTH), BF16)
    return pl.pallas_call(
        _mem_kv_kernel,
        out_shape=(shape, shape),
        grid=(batch,),
        in_specs=[pl.BlockSpec((1, m, d), lambda b: (b, 0, 0)),
                  pl.BlockSpec(g.shape, const),
                  pl.BlockSpec(w.shape, const),
                  pl.BlockSpec(gk.shape, const)],
        out_specs=(pl.BlockSpec((1, m, MEM_WIDTH), lambda b: (b, 0, 0)),
                   pl.BlockSpec((1, m, MEM_WIDTH), lambda b: (b, 0, 0))),
        compiler_params=_params("arbitrary"),
        name="mem_kv",
    )(mem, g, w, gk)


def _mem_attn_kernel(h_ref, w_ref, k_ref, v_ref, gq_ref, o_ref):
    p = _dot_nt(h_ref[...], w_ref[...])
    gq = gq_ref[...] * (MEM_HEAD_DIM ** -0.5)
    for hd in range(MEM_HEADS):
        lo = hd * MEM_HEAD_DIM
        q = (_rms(p[:, lo:lo + MEM_HEAD_DIM], MEM_HEAD_DIM) * gq).astype(BF16)
        z = p[:, MEM_WIDTH + lo:MEM_WIDTH + lo + MEM_HEAD_DIM]
        s = _dot_nt(q, k_ref[0, :, lo:lo + MEM_HEAD_DIM])
        e = jnp.exp(s - jnp.max(s, axis=-1, keepdims=True))
        y = _dot(e.astype(BF16), v_ref[0, :, lo:lo + MEM_HEAD_DIM])
        y = y * (1.0 / jnp.sum(e, axis=-1, keepdims=True))
        o_ref[:, lo:lo + MEM_HEAD_DIM] = (y * (z * _sigmoid(z))).astype(o_ref.dtype)


def _mem_attention(h, w, col0, k_mem, v_mem, gq, batch, seq, tm):
    t, d = h.shape
    ns = seq // tm
    m = k_mem.shape[1]
    const = lambda b, s: (0, 0)
    return pl.pallas_call(
        _mem_attn_kernel,
        out_shape=jax.ShapeDtypeStruct((t, MEM_WIDTH), BF16),
        grid=(batch, ns),
        in_specs=[pl.BlockSpec((tm, d), lambda b, s: (b * ns + s, 0)),
                  pl.BlockSpec((pl.Element(2 * MEM_WIDTH), pl.Element(d)), lambda b, s: (col0, 0)),
                  pl.BlockSpec((1, m, MEM_WIDTH), lambda b, s: (b, 0, 0)),
                  pl.BlockSpec((1, m, MEM_WIDTH), lambda b, s: (b, 0, 0)),
                  pl.BlockSpec(gq.shape, const)],
        out_specs=pl.BlockSpec((tm, MEM_WIDTH), lambda b, s: (b * ns + s, 0)),
        compiler_params=_params("arbitrary", "arbitrary"),
        name="mem_attention",
    )(h, w, k_mem, v_mem, gq)


def _gate_kernel(h_ref, w_ref, o_ref, *, silu):
    p = _dot_nt(h_ref[...], w_ref[...])
    s = _sigmoid(p)
    o_ref[...] = ((p * s) if silu else s).astype(o_ref.dtype)


def _gate_proj(h, w, col0, n, silu, tm, tn):
    t, d = h.shape
    assert col0 % BF16_SUBLANES == 0 and n % tn == 0
    return pl.pallas_call(
        functools.partial(_gate_kernel, silu=silu),
        out_shape=jax.ShapeDtypeStruct((t, n), BF16),
        grid=(t // tm, n // tn),
        in_specs=[pl.BlockSpec((tm, d), lambda i, j: (i, 0)),
                  pl.BlockSpec((pl.Element(tn), pl.Element(d)),
                               lambda i, j: (pl.multiple_of(col0 + j * tn, BF16_SUBLANES), 0))],
        out_specs=pl.BlockSpec((tm, tn), lambda i, j: (i, j)),
        compiler_params=_params("arbitrary", "arbitrary"),
        name="silu_gate_proj" if silu else "sigmoid_gate_proj",
    )(h, w)


def _merge_out_kernel(ac_ref, am_ref, ae_ref, gc_ref, gm_ref, ge_ref, x_ref, wc_ref, wm_ref, we_ref,
                      wo_ref, o_ref):
    merged = (gc_ref[...].astype(F32) * _dot(ac_ref[...], wc_ref[...])
              + gm_ref[...].astype(F32) * _dot(am_ref[...], wm_ref[...])
              + ge_ref[...].astype(F32) * _dot(ae_ref[...], we_ref[...]))
    o_ref[...] = x_ref[...] + _dot(merged.astype(BF16), wo_ref[...])


def _merge_out(a_conv, a_mla, a_mem, gates, x2d, w_conv, w_mla, w_mem, w_o, tm):
    t, d = x2d.shape
    row = lambda a: pl.BlockSpec((tm, a.shape[1]), lambda i: (i, 0))
    gate = lambda k: pl.BlockSpec((tm, d), lambda i: (i, k))
    wgt = lambda w: pl.BlockSpec(w.shape, lambda i: (0, 0), pipeline_mode=pl.Buffered(1))
    return pl.pallas_call(
        _merge_out_kernel,
        out_shape=jax.ShapeDtypeStruct((t, d), F32),
        grid=(t // tm,),
        in_specs=[row(a_conv), row(a_mla), row(a_mem), gate(0), gate(1), gate(2), row(x2d),
                  wgt(w_conv), wgt(w_mla), wgt(w_mem), wgt(w_o)],
        out_specs=pl.BlockSpec((tm, d), lambda i: (i, 0)),
        compiler_params=pltpu.CompilerParams(dimension_semantics=("arbitrary",),
                                             vmem_limit_bytes=MERGE_VMEM_LIMIT_BYTES),
        name="merge_out",
    )(a_conv, a_mla, a_mem, gates, gates, gates, x2d, w_conv, w_mla, w_mem, w_o)


def _layer(x, cost, sint, mem, norm_g, w_in, conv_w, w_conv_out, mla_q_norm_g, w_uq, mla_kv_norm_g,
           w_ukv, mla_qn_nope_g, mla_qn_rope_g, mla_kn_nope_g, mla_kn_rope_g, w_mla_out, mem_norm_g,
           w_mem_kv, mem_qn_g, mem_kn_g, w_mem_out, w_o):
    batch, seq, d = x.shape
    t = batch * seq
    x2d = x.reshape(t, d)

    o_rest = 4 * CONV_WIDTH + Q_LORA + KV_LORA + QK_ROPE
    w_front = w_in.T.astype(BF16)
    w_uqt = w_uq.astype(BF16).T
    w_ukv_h = w_ukv.astype(BF16).reshape(KV_LORA, MLA_HEADS, QK_NOPE + V_HEAD)
    w_kn = w_ukv_h[:, :, :QK_NOPE].reshape(KV_LORA, MLA_HEADS * QK_NOPE)
    w_vt = w_ukv_h[:, :, QK_NOPE:].reshape(KV_LORA, MLA_WIDTH).T
    qscale = (QK_NOPE + QK_ROPE) ** -0.5 * LOG2_E

    conv_act, h = _norm_conv_branch(x2d, norm_g[None, :], w_front, conv_w, seq, tm=1024)
    qt, k, vt = _mla_prep(h, w_front, w_uqt, w_kn, w_vt, mla_q_norm_g[None, :],
                          mla_kv_norm_g[None, :], (mla_qn_nope_g * qscale)[:, None],
                          (mla_qn_rope_g * qscale)[:, None], mla_kn_nope_g[None, :],
                          mla_kn_rope_g[:, None], cost, sint, batch, seq, tm=256, tk=ATTN_TILE)
    sz_mla = _gate_proj(h, w_front, o_rest, MLA_WIDTH, True, tm=1024, tn=2048)
    gates = _gate_proj(h, w_front, o_rest + MLA_WIDTH + 2 * MEM_WIDTH, 3 * d, False, tm=1024, tn=2048)
    mla_act = _mla_attention(qt, k, vt, sz_mla.reshape(batch, seq, MLA_WIDTH), tq=ATTN_TILE, heads=2)
    k_mem, v_mem = _mem_kv(mem, mem_norm_g[None, :], w_mem_kv.astype(BF16), mem_kn_g[None, :])
    mem_act = _mem_attention(h, w_front, o_rest + MLA_WIDTH, k_mem, v_mem, mem_qn_g[None, :], batch, seq,
                             tm=512)
    out = _merge_out(conv_act, mla_act.reshape(t, MLA_WIDTH), mem_act, gates, x2d, w_conv_out.astype(BF16),
                     w_mla_out.astype(BF16), w_mem_out.astype(BF16), w_o.astype(BF16), tm=256)
    return out.reshape(batch, seq, d)


def kernel(x, positions, mem, norm_g, w_in, conv_w, w_conv_out, mla_q_norm_g, w_uq, mla_kv_norm_g, w_ukv, mla_qn_nope_g, mla_qn_rope_g, mla_kn_nope_g, mla_kn_rope_g, w_mla_out, mem_norm_g, w_mem_kv, mem_qn_g, mem_kn_g, w_mem_out, w_o):
    batch, seq, _ = x.shape
    half = QK_ROPE // 2
    inv_freq = jnp.power(ROPE_THETA, -jnp.arange(half, dtype=F32) / half)
    ang = positions.astype(F32)[..., None] * inv_freq
    cost = jnp.cos(ang).reshape(batch * seq, half).T
    sint = jnp.sin(ang).reshape(batch * seq, half).T
    for l in range(norm_g.shape[0]):
        x = _layer(x, cost, sint, mem, norm_g[l], w_in[l], conv_w[l], w_conv_out[l], mla_q_norm_g[l],
                   w_uq[l], mla_kv_norm_g[l], w_ukv[l], mla_qn_nope_g[l], mla_qn_rope_g[l],
                   mla_kn_nope_g[l], mla_kn_rope_g[l], w_mla_out[l], mem_norm_g[l], w_mem_kv[l],
                   mem_qn_g[l], mem_kn_g[l], w_mem_out[l], w_o[l])
    return x
```

```python
import functools

import jax
import jax.numpy as jnp
from jax import lax
from jax.experimental import pallas as pl
from jax.experimental.pallas import tpu as pltpu

F32 = jnp.float32
BF16 = jnp.bfloat16

EPS = 1e-6
CHUNK = 64
CONV_WIDTH = 1024
CONV_K = 3
CONV_GROUP = 256
MLA_HEADS = 16
Q_LORA = 512
KV_LORA = 512
QK_NOPE = 128
QK_ROPE = 64
V_HEAD = 128
MLA_WIDTH = MLA_HEADS * V_HEAD
HEAD_PAD = 256
ROPE_THETA = 10000.0
MEM_HEADS = 4
MEM_HEAD_DIM = 256
MEM_WIDTH = MEM_HEADS * MEM_HEAD_DIM
NEG_BIG = -1e30
LOG2_E = 1.4426950408889634
ATTN_TILE = 512
ATTN_GROUP = 4
ATTN_HEADS_PER_STEP = 2

PROJ_ROW_TILE = 1024
GATE_COL_TILE = 2048
MEM_ROW_TILE = 1024
PREP_ROW_TILE = 256
MERGE_ROW_TILE = 256

SUBLANES = 8
BF16_SUBLANES = 16
VMEM_LIMIT_BYTES = 52 * 1024 * 1024
MERGE_VMEM_LIMIT_BYTES = 50 * 1024 * 1024


def _params(*sem):
    return pltpu.CompilerParams(dimension_semantics=sem, vmem_limit_bytes=VMEM_LIMIT_BYTES)


def _rms(x, width):
    ms = jnp.sum(x * x, axis=-1, keepdims=True) * (1.0 / width)
    return x * lax.rsqrt(ms + EPS)


def _sigmoid(x):
    return 0.5 * jnp.tanh(0.5 * x) + 0.5


def _dot(a, b):
    return jnp.dot(a, b, preferred_element_type=F32)


def _dot_nt(a, b):
    return lax.dot_general(a, b, (((1,), (1,)), ((), ())), preferred_element_type=F32)


def _conv_kernel(x_ref, ng_ref, wc_ref, wb_ref, wu_ref, wz_ref, cw_ref, o_ref, h_ref,
                 carry_ref, buf_ref, *, tm, tiles_per_seq):
    i = pl.program_id(0)
    g = pl.program_id(1)

    @pl.when(g == 0)
    def _():
        x = x_ref[...]
        h_ref[...] = (_rms(x, x.shape[-1]) * ng_ref[...]).astype(BF16)

    h = h_ref[...]
    b = _dot_nt(h, wb_ref[...])
    z = _dot_nt(h, wz_ref[...])
    cu = _dot_nt(h, wc_ref[...]) * _dot_nt(h, wu_ref[...])
    seq_start = (i % tiles_per_seq) == 0
    buf_ref[0:SUBLANES, :] = jnp.where(seq_start, 0.0, carry_ref[g])
    buf_ref[SUBLANES:SUBLANES + tm, :] = cu
    carry_ref[g] = cu[tm - SUBLANES:tm, :]
    cw = cw_ref[...]
    y = cu * cw[CONV_K - 1:CONV_K, :]
    for tap in range(CONV_K - 1):
        lag = CONV_K - 1 - tap
        y = y + buf_ref[SUBLANES - lag:SUBLANES - lag + tm, :] * cw[tap:tap + 1, :]
    o_ref[...] = (b * y * (z * _sigmoid(z))).astype(o_ref.dtype)


def _norm_conv_branch(x2d, norm_g, w_front, conv_w, seq, tm):
    t, d = x2d.shape
    n_groups = CONV_WIDTH // CONV_GROUP
    kern = functools.partial(_conv_kernel, tm=tm, tiles_per_seq=seq // tm)
    part = lambda k: pl.BlockSpec((CONV_GROUP, d), lambda i, g: (k * n_groups + g, 0))
    return pl.pallas_call(
        kern,
        out_shape=(jax.ShapeDtypeStruct((t, CONV_WIDTH), BF16), jax.ShapeDtypeStruct((t, d), BF16)),
        grid=(t // tm, n_groups),
        in_specs=[pl.BlockSpec((tm, d), lambda i, g: (i, 0)),
                  pl.BlockSpec((1, d), lambda i, g: (0, 0)),
                  part(0), part(1), part(2), part(3),
                  pl.BlockSpec((CONV_K, CONV_GROUP), lambda i, g: (0, g))],
        out_specs=(pl.BlockSpec((tm, CONV_GROUP), lambda i, g: (i, g)),
                   pl.BlockSpec((tm, d), lambda i, g: (i, 0))),
        scratch_shapes=[pltpu.VMEM((n_groups, SUBLANES, CONV_GROUP), F32),
                        pltpu.VMEM((tm + SUBLANES, CONV_GROUP), F32)],
        compiler_params=_params("arbitrary", "arbitrary"),
        name="norm_conv_branch",
    )(x2d, norm_g, w_front, w_front, w_front, w_front, conv_w)


def _rope_t(x, g, cost, sint):
    half = QK_ROPE // 2
    x = x * lax.rsqrt(jnp.sum(x * x, axis=0, keepdims=True) * (1.0 / QK_ROPE) + EPS) * g
    x1 = x[0:half, :]
    x2 = x[half:QK_ROPE, :]
    return x1 * cost - x2 * sint, x2 * cost + x1 * sint


def _mla_prep_kernel(h_ref, win_ref, wkrt_ref, wuqt_ref, wkn_ref, wvt_ref, gq_ref, gkv_ref, gqn_ref,
                     gqr_ref, gkn_ref, gkr_ref, cost_ref, sint_ref, qt_ref, k_ref, vt_ref):
    tm = h_ref.shape[0]
    h = h_ref[...]
    p = _dot_nt(h, win_ref[...])
    cqn = (_rms(p[:, 0:Q_LORA], Q_LORA) * gq_ref[...]).astype(BF16)
    ckvn = (_rms(p[:, Q_LORA:Q_LORA + KV_LORA], KV_LORA) * gkv_ref[...]).astype(BF16)
    qt = _dot_nt(wuqt_ref[...], cqn)
    kn = _dot(ckvn, wkn_ref[...])
    vt = _dot_nt(wvt_ref[...], ckvn)
    krt = _dot_nt(wkrt_ref[...], h)

    cost = cost_ref[...]
    sint = sint_ref[...]
    kr1, kr2 = _rope_t(krt, jnp.broadcast_to(gkr_ref[...], (QK_ROPE, tm)), cost, sint)
    zeros_f = jnp.zeros((HEAD_PAD - QK_NOPE - QK_ROPE, tm), F32)
    k_rope = jnp.concatenate([kr1, kr2, zeros_f], axis=0).T.astype(BF16)

    gqn = jnp.broadcast_to(gqn_ref[...], (QK_NOPE, tm))
    gqr = jnp.broadcast_to(gqr_ref[...], (QK_ROPE, tm))
    gkn = gkn_ref[...]
    zeros = zeros_f.astype(BF16)
    for hd in range(MLA_HEADS):
        lo = hd * (QK_NOPE + QK_ROPE)
        qn = qt[lo:lo + QK_NOPE, :]
        qr = qt[lo + QK_NOPE:lo + QK_NOPE + QK_ROPE, :]
        qn = qn * lax.rsqrt(jnp.sum(qn * qn, axis=0, keepdims=True) * (1.0 / QK_NOPE) + EPS) * gqn
        r1, r2 = _rope_t(qr, gqr, cost, sint)
        half = QK_ROPE // 2
        qt_ref[0, hd, 0:QK_NOPE, :] = qn.astype(BF16)
        qt_ref[0, hd, QK_NOPE:QK_NOPE + half, :] = r1.astype(BF16)
        qt_ref[0, hd, QK_NOPE + half:QK_NOPE + QK_ROPE, :] = r2.astype(BF16)
        qt_ref[0, hd, QK_NOPE + QK_ROPE:HEAD_PAD, :] = zeros
        kh = kn[:, hd * QK_NOPE:(hd + 1) * QK_NOPE]
        k_ref[0, hd, :, 0:QK_NOPE] = (_rms(kh, QK_NOPE) * gkn).astype(BF16)
        k_ref[0, hd, :, QK_NOPE:HEAD_PAD] = k_rope
        vt_ref[0, hd, 0, :, :] = vt[hd * V_HEAD:(hd + 1) * V_HEAD, :].astype(BF16)


def _mla_prep(h, w_front, w_uqt, w_kn, w_vt, gq, gkv, gqn, gqr, gkn, gkr, cost, sint,
              batch, seq, tm, tk):
    t, d = h.shape
    ns = seq // tm
    per_chunk = tk // tm
    lora = Q_LORA + KV_LORA
    const = lambda b, s: (0, 0)
    row = lambda b, s: (b * ns + s, 0)
    col = lambda b, s: (0, b * ns + s)
    qt_shape = jax.ShapeDtypeStruct((batch, MLA_HEADS, HEAD_PAD, seq), BF16)
    k_shape = jax.ShapeDtypeStruct((batch, MLA_HEADS, seq, HEAD_PAD), BF16)
    vt_shape = jax.ShapeDtypeStruct((batch, MLA_HEADS, seq // tk, V_HEAD, tk), BF16)
    full = lambda a: pl.BlockSpec(a.shape, const)
    return pl.pallas_call(
        _mla_prep_kernel,
        out_shape=(qt_shape, k_shape, vt_shape),
        grid=(batch, ns),
        in_specs=[pl.BlockSpec((tm, d), row),
                  pl.BlockSpec((lora, d), lambda b, s: (4 * CONV_WIDTH // lora, 0)),
                  pl.BlockSpec((QK_ROPE, d), lambda b, s: ((4 * CONV_WIDTH + lora) // QK_ROPE, 0)),
                  full(w_uqt), full(w_kn), full(w_vt),
                  full(gq), full(gkv), full(gqn), full(gqr), full(gkn), full(gkr),
                  pl.BlockSpec((QK_ROPE // 2, tm), col), pl.BlockSpec((QK_ROPE // 2, tm), col)],
        out_specs=(pl.BlockSpec((1, MLA_HEADS, HEAD_PAD, tm), lambda b, s: (b, 0, 0, s)),
                   pl.BlockSpec((1, MLA_HEADS, tm, HEAD_PAD), lambda b, s: (b, 0, s, 0)),
                   pl.BlockSpec((1, MLA_HEADS, 1, V_HEAD, tm),
                                lambda b, s: (b, 0, s // per_chunk, 0, s % per_chunk))),
        compiler_params=_params("arbitrary", "arbitrary"),
        name="mla_prep",
    )(h, w_front, w_front, w_uqt, w_kn, w_vt, gq, gkv, gqn, gqr, gkn, gkr, cost, sint)


def _attn_kernel(qt_ref, qtn_ref, k_ref, vt_ref, sz_ref, o_ref, acc_ref, m_ref, l_ref, s0_ref, s1_ref,
                 *, tq, heads):
    j = pl.program_id(2)
    s_bufs = (s0_ref, s1_ref)

    def scores(hh, ki, q, dst):
        start = pl.multiple_of(ki * tq, tq)
        s = _dot(k_ref[0, hh, pl.ds(start, tq), :], q)
        dst[hh, 0:tq, :] = s
        dst[hh, tq:tq + 1, :] = jnp.max(s, axis=0, keepdims=True)

    def update(hh, ki, src, diagonal):
        s = src[hh, 0:tq, :]
        if diagonal:
            kc = lax.broadcasted_iota(jnp.int32, s.shape, 0) // CHUNK
            qc = lax.broadcasted_iota(jnp.int32, s.shape, 1) // CHUNK
            s = jnp.where(kc <= qc, s, NEG_BIG)
            m_tile = jnp.max(s, axis=0, keepdims=True)
        else:
            m_tile = src[hh, tq:tq + 1, :]
        m = m_ref[hh]
        m_new = jnp.maximum(m, m_tile)
        alpha = jnp.exp2(m - m_new)
        p = jnp.exp2(s - m_new)
        m_ref[hh] = m_new
        l_ref[hh] = alpha * l_ref[hh] + jnp.sum(p, axis=0, keepdims=True)
        acc_ref[hh] = alpha * acc_ref[hh] + _dot(vt_ref[0, hh, ki], p.astype(BF16))

    @pl.when(j == 0)
    def _():
        for hh in range(heads):
            scores(hh, 0, qt_ref[0, hh, :, 0:tq], s0_ref)

    for u in range(ATTN_GROUP):
        q_tile = ATTN_GROUP * j + u
        first = ((u + 1) // 2) % 2
        bufs = (s_bufs[first], s_bufs[1 - first])

        def q_cur(hh, u=u):
            return qt_ref[0, hh, :, u * tq:(u + 1) * tq]

        def q_next(hh, u=u):
            if u + 1 < ATTN_GROUP:
                return qt_ref[0, hh, :, (u + 1) * tq:(u + 2) * tq]
            return qtn_ref[0, hh]

        acc_ref[...] = jnp.zeros_like(acc_ref)
        l_ref[...] = jnp.zeros_like(l_ref)
        m_ref[...] = jnp.full(m_ref.shape, NEG_BIG, F32)

        def step(ki, parity, bufs=bufs, q_cur=q_cur):
            for hh in range(heads):
                scores(hh, ki + 1, q_cur(hh), bufs[1 - parity])
                update(hh, ki, bufs[parity], False)

        def group(jj, carry, step=step):
            for r in range(ATTN_GROUP):
                step(ATTN_GROUP * jj + r, r % 2)
            return carry

        lax.fori_loop(0, j, group, 0)
        for r in range(u):
            step(ATTN_GROUP * j + r, r % 2)
        for hh in range(heads):
            scores(hh, 0, q_next(hh), bufs[1 - u % 2])
            update(hh, q_tile, bufs[u % 2], True)
            y = (acc_ref[hh] * (1.0 / l_ref[hh])).T
            sz = sz_ref[0, u * tq:(u + 1) * tq, hh * V_HEAD:(hh + 1) * V_HEAD].astype(F32)
            o_ref[0, u * tq:(u + 1) * tq, hh * V_HEAD:(hh + 1) * V_HEAD] = (y * sz).astype(o_ref.dtype)


def _mla_attention(qt, k, vt, sz, tq, heads):
    batch, n_heads, seq, _ = k.shape
    assert ATTN_GROUP % 4 == 0
    kern = functools.partial(_attn_kernel, tq=tq, heads=heads)
    tg = ATTN_GROUP * tq
    last = seq // tq - 1
    return pl.pallas_call(
        kern,
        out_shape=jax.ShapeDtypeStruct((batch, seq, MLA_WIDTH), BF16),
        grid=(batch, n_heads // heads, seq // tg),
        in_specs=[pl.BlockSpec((1, heads, HEAD_PAD, tg), lambda b, h, i: (b, h, 0, i)),
                  pl.BlockSpec((1, heads, HEAD_PAD, tq),
                               lambda b, h, i: (b, h, 0, jnp.minimum(ATTN_GROUP * (i + 1), last))),
                  pl.BlockSpec((1, heads, seq, HEAD_PAD), lambda b, h, i: (b, h, 0, 0)),
                  pl.BlockSpec((1, heads, seq // tq, V_HEAD, tq), lambda b, h, i: (b, h, 0, 0, 0)),
                  pl.BlockSpec((1, tg, heads * V_HEAD), lambda b, h, i: (b, i, h))],
        out_specs=pl.BlockSpec((1, tg, heads * V_HEAD), lambda b, h, i: (b, i, h)),
        scratch_shapes=[pltpu.VMEM((heads, V_HEAD, tq), F32),
                        pltpu.VMEM((heads, 1, tq), F32),
                        pltpu.VMEM((heads, 1, tq), F32),
                        pltpu.VMEM((heads, tq + SUBLANES, tq), F32),
                        pltpu.VMEM((heads, tq + SUBLANES, tq), F32)],
        compiler_params=_params("arbitrary", "arbitrary", "arbitrary"),
        name="mla_attention",
    )(qt, qt, k, vt, sz)


def _mem_kv_kernel(mem_ref, g_ref, w_ref, gk_ref, k_ref, v_ref):
    m = mem_ref[0]
    kv = _dot((_rms(m, m.shape[-1]) * g_ref[...]).astype(BF16), w_ref[...])
    gk = gk_ref[...]
    for hd in range(MEM_HEADS):
        lo = hd * MEM_HEAD_DIM
        k_ref[0, :, lo:lo + MEM_HEAD_DIM] = (_rms(kv[:, lo:lo + MEM_HEAD_DIM], MEM_HEAD_DIM) * gk).astype(BF16)
    v_ref[0] = kv[:, MEM_WIDTH:].astype(BF16)


def _mem_kv(mem, g, w, gk):
    batch, m, d = mem.shape
    const = lambda b: (0, 0)
    shape = jax.ShapeDtypeStruct((batch, m, MEM_WIDTH), BF16)
    return pl.pallas_call(
        _mem_kv_kernel,
        out_shape=(shape, shape),
        grid=(batch,),
        in_specs=[pl.BlockSpec((1, m, d), lambda b: (b, 0, 0)),
                  pl.BlockSpec(g.shape, const),
                  pl.BlockSpec(w.shape, const),
                  pl.BlockSpec(gk.shape, const)],
        out_specs=(pl.BlockSpec((1, m, MEM_WIDTH), lambda b: (b, 0, 0)),
                   pl.BlockSpec((1, m, MEM_WIDTH), lambda b: (b, 0, 0))),
        compiler_params=_params("arbitrary"),
        name="mem_kv",
    )(mem, g, w, gk)


def _mem_attn_kernel(h_ref, w_ref, k_ref, v_ref, gq_ref, o_ref):
    p = _dot_nt(h_ref[...], w_ref[...])
    gq = gq_ref[...] * (MEM_HEAD_DIM ** -0.5)
    for hd in range(MEM_HEADS):
        lo = hd * MEM_HEAD_DIM
        q = (_rms(p[:, lo:lo + MEM_HEAD_DIM], MEM_HEAD_DIM) * gq).astype(BF16)
        z = p[:, MEM_WIDTH + lo:MEM_WIDTH + lo + MEM_HEAD_DIM]
        s = _dot_nt(q, k_ref[0, :, lo:lo + MEM_HEAD_DIM])
        e = jnp.exp(s - jnp.max(s, axis=-1, keepdims=True))
        y = _dot(e.astype(BF16), v_ref[0, :, lo:lo + MEM_HEAD_DIM])
        y = y * (1.0 / jnp.sum(e, axis=-1, keepdims=True))
        o_ref[:, lo:lo + MEM_HEAD_DIM] = (y * (z * _sigmoid(z))).astype(o_ref.dtype)


def _mem_attention(h, w, col0, k_mem, v_mem, gq, batch, seq, tm):
    t, d = h.shape
    ns = seq // tm
    m = k_mem.shape[1]
    const = lambda b, s: (0, 0)
    return pl.pallas_call(
        _mem_attn_kernel,
        out_shape=jax.ShapeDtypeStruct((t, MEM_WIDTH), BF16),
        grid=(batch, ns),
        in_specs=[pl.BlockSpec((tm, d), lambda b, s: (b * ns + s, 0)),
                  pl.BlockSpec((pl.Element(2 * MEM_WIDTH), pl.Element(d)), lambda b, s: (col0, 0)),
                  pl.BlockSpec((1, m, MEM_WIDTH), lambda b, s: (b, 0, 0)),
                  pl.BlockSpec((1, m, MEM_WIDTH), lambda b, s: (b, 0, 0)),
                  pl.BlockSpec(gq.shape, const)],
        out_specs=pl.BlockSpec((tm, MEM_WIDTH), lambda b, s: (b * ns + s, 0)),
        compiler_params=_params("arbitrary", "arbitrary"),
        name="mem_attention",
    )(h, w, k_mem, v_mem, gq)


def _gate_kernel(h_ref, w_ref, o_ref, *, silu):
    p = _dot_nt(h_ref[...], w_ref[...])
    s = _sigmoid(p)
    o_ref[...] = ((p * s) if silu else s).astype(o_ref.dtype)


def _gate_proj(h, w, col0, n, silu, tm, tn):
    t, d = h.shape
    assert col0 % BF16_SUBLANES == 0 and n % tn == 0
    return pl.pallas_call(
        functools.partial(_gate_kernel, silu=silu),
        out_shape=jax.ShapeDtypeStruct((t, n), BF16),
        grid=(t // tm, n // tn),
        in_specs=[pl.BlockSpec((tm, d), lambda i, j: (i, 0)),
                  pl.BlockSpec((pl.Element(tn), pl.Element(d)),
                               lambda i, j: (pl.multiple_of(col0 + j * tn, BF16_SUBLANES), 0))],
        out_specs=pl.BlockSpec((tm, tn), lambda i, j: (i, j)),
        compiler_params=_params("arbitrary", "arbitrary"),
        name="silu_gate_proj" if silu else "sigmoid_gate_proj",
    )(h, w)


def _merge_out_kernel(ac_ref, am_ref, ae_ref, gc_ref, gm_ref, ge_ref, x_ref, wc_ref, wm_ref, we_ref,
                      wo_ref, o_ref):
    merged = (gc_ref[...].astype(F32) * _dot(ac_ref[...], wc_ref[...])
              + gm_ref[...].astype(F32) * _dot(am_ref[...], wm_ref[...])
              + ge_ref[...].astype(F32) * _dot(ae_ref[...], we_ref[...]))
    o_ref[...] = x_ref[...] + _dot(merged.astype(BF16), wo_ref[...])


def _merge_out(a_conv, a_mla, a_mem, gates, x2d, w_conv, w_mla, w_mem, w_o, tm):
    t, d = x2d.shape
    row = lambda a: pl.BlockSpec((tm, a.shape[1]), lambda i: (i, 0))
    gate = lambda k: pl.BlockSpec((tm, d), lambda i: (i, k))
    wgt = lambda w: pl.BlockSpec(w.shape, lambda i: (0, 0), pipeline_mode=pl.Buffered(1))
    return pl.pallas_call(
        _merge_out_kernel,
        out_shape=jax.ShapeDtypeStruct((t, d), F32),
        grid=(t // tm,),
        in_specs=[row(a_conv), row(a_mla), row(a_mem), gate(0), gate(1), gate(2), row(x2d),
                  wgt(w_conv), wgt(w_mla), wgt(w_mem), wgt(w_o)],
        out_specs=pl.BlockSpec((tm, d), lambda i: (i, 0)),
        compiler_params=pltpu.CompilerParams(dimension_semantics=("arbitrary",),
                                             vmem_limit_bytes=MERGE_VMEM_LIMIT_BYTES),
        name="merge_out",
    )(a_conv, a_mla, a_mem, gates, gates, gates, x2d, w_conv, w_mla, w_mem, w_o)


def _layer(x, cost, sint, mem, norm_g, w_in, conv_w, w_conv_out, mla_q_norm_g, w_uq, mla_kv_norm_g,
           w_ukv, mla_qn_nope_g, mla_qn_rope_g, mla_kn_nope_g, mla_kn_rope_g, w_mla_out, mem_norm_g,
           w_mem_kv, mem_qn_g, mem_kn_g, w_mem_out, w_o):
    batch, seq, d = x.shape
    t = batch * seq
    x2d = x.reshape(t, d)

    o_rest = 4 * CONV_WIDTH + Q_LORA + KV_LORA + QK_ROPE
    w_front = w_in.T.astype(BF16)
    w_uqt = w_uq.astype(BF16).T
    w_ukv_h = w_ukv.astype(BF16).reshape(KV_LORA, MLA_HEADS, QK_NOPE + V_HEAD)
    w_kn = w_ukv_h[:, :, :QK_NOPE].reshape(KV_LORA, MLA_HEADS * QK_NOPE)
    w_vt = w_ukv_h[:, :, QK_NOPE:].reshape(KV_LORA, MLA_WIDTH).T
    qscale = (QK_NOPE + QK_ROPE) ** -0.5 * LOG2_E

    conv_act, h = _norm_conv_branch(x2d, norm_g[None, :], w_front, conv_w, seq, tm=PROJ_ROW_TILE)
    qt, k, vt = _mla_prep(h, w_front, w_uqt, w_kn, w_vt, mla_q_norm_g[None, :],
                          mla_kv_norm_g[None, :], (mla_qn_nope_g * qscale)[:, None],
                          (mla_qn_rope_g * qscale)[:, None], mla_kn_nope_g[None, :],
                          mla_kn_rope_g[:, None], cost, sint, batch, seq, tm=PREP_ROW_TILE, tk=ATTN_TILE)
    sz_mla = _gate_proj(h, w_front, o_rest, MLA_WIDTH, True, tm=PROJ_ROW_TILE, tn=GATE_COL_TILE)
    gates = _gate_proj(h, w_front, o_rest + MLA_WIDTH + 2 * MEM_WIDTH, 3 * d, False, tm=PROJ_ROW_TILE,
                       tn=GATE_COL_TILE)
    mla_act = _mla_attention(qt, k, vt, sz_mla.reshape(batch, seq, MLA_WIDTH), tq=ATTN_TILE,
                             heads=ATTN_HEADS_PER_STEP)
    k_mem, v_mem = _mem_kv(mem, mem_norm_g[None, :], w_mem_kv.astype(BF16), mem_kn_g[None, :])
    mem_act = _mem_attention(h, w_front, o_rest + MLA_WIDTH, k_mem, v_mem, mem_qn_g[None, :], batch, seq,
                             tm=MEM_ROW_TILE)
    out = _merge_out(conv_act, mla_act.reshape(t, MLA_WIDTH), mem_act, gates, x2d, w_conv_out.astype(BF16),
                     w_mla_out.astype(BF16), w_mem_out.astype(BF16), w_o.astype(BF16), tm=MERGE_ROW_TILE)
    return out.reshape(batch, seq, d)


def kernel(x, positions, mem, norm_g, w_in, conv_w, w_conv_out, mla_q_norm_g, w_uq, mla_kv_norm_g, w_ukv, mla_qn_nope_g, mla_qn_rope_g, mla_kn_nope_g, mla_kn_rope_g, w_mla_out, mem_norm_g, w_mem_kv, mem_qn_g, mem_kn_g, w_mem_out, w_o):
    batch, seq, _ = x.shape
    half = QK_ROPE // 2
    inv_freq = jnp.power(ROPE_THETA, -jnp.arange(half, dtype=F32) / half)
    ang = positions.astype(F32)[..., None] * inv_freq
    cost = jnp.cos(ang).reshape(batch * seq, half).T
    sint = jnp.sin(ang).reshape(batch * seq, half).T
    for l in range(norm_g.shape[0]):
        x = _layer(x, cost, sint, mem, norm_g[l], w_in[l], conv_w[l], w_conv_out[l], mla_q_norm_g[l],
                   w_uq[l], mla_kv_norm_g[l], w_ukv[l], mla_qn_nope_g[l], mla_qn_rope_g[l],
                   mla_kn_nope_g[l], mla_kn_rope_g[l], w_mla_out[l], mem_norm_g[l], w_mem_kv[l],
                   mem_qn_g[l], mem_kn_g[l], w_mem_out[l], w_o[l])
    return x
```

```python
import functools

import jax
import jax.numpy as jnp
from jax import lax
from jax.experimental import pallas as pl
from jax.experimental.pallas import tpu as pltpu

F32 = jnp.float32
BF16 = jnp.bfloat16

EPS = 1e-6
CHUNK = 64
CONV_WIDTH = 1024
CONV_K = 3
CONV_GROUP = 256
MLA_HEADS = 16
Q_LORA = 512
KV_LORA = 512
QK_NOPE = 128
QK_ROPE = 64
V_HEAD = 128
MLA_WIDTH = MLA_HEADS * V_HEAD
HEAD_PAD = 256
ROPE_THETA = 10000.0
MEM_HEADS = 4
MEM_HEAD_DIM = 256
MEM_WIDTH = MEM_HEADS * MEM_HEAD_DIM
NEG_BIG = -1e30
LOG2_E = 1.4426950408889634
ATTN_TILE = 512
ATTN_GROUP = 4
ATTN_HEADS_PER_STEP = 2

PROJ_ROW_TILE = 1024
GATE_COL_TILE = 2048
MEM_ROW_TILE = 1024
PREP_ROW_TILE = 256
MERGE_ROW_TILE = 256

SUBLANES = 8
BF16_SUBLANES = 16
VMEM_LIMIT_BYTES = 52 * 1024 * 1024
MERGE_VMEM_LIMIT_BYTES = 50 * 1024 * 1024


def _params(*sem):
    return pltpu.CompilerParams(dimension_semantics=sem, vmem_limit_bytes=VMEM_LIMIT_BYTES)


def _rms(x, width):
    ms = jnp.sum(x * x, axis=-1, keepdims=True) * (1.0 / width)
    return x * lax.rsqrt(ms + EPS)


def _sigmoid(x):
    return 0.5 * jnp.tanh(0.5 * x) + 0.5


def _dot(a, b):
    return jnp.dot(a, b, preferred_element_type=F32)


def _dot_nt(a, b):
    return lax.dot_general(a, b, (((1,), (1,)), ((), ())), preferred_element_type=F32)


def _conv_kernel(x_ref, ng_ref, wc_ref, wb_ref, wu_ref, wz_ref, cw_ref, o_ref, h_ref,
                 carry_ref, buf_ref, *, tm, tiles_per_seq):
    i = pl.program_id(0)
    g = pl.program_id(1)

    @pl.when(g == 0)
    def _():
        x = x_ref[...]
        h_ref[...] = (_rms(x, x.shape[-1]) * ng_ref[...]).astype(BF16)

    h = h_ref[...]
    b = _dot_nt(h, wb_ref[...])
    z = _dot_nt(h, wz_ref[...])
    cu = _dot_nt(h, wc_ref[...]) * _dot_nt(h, wu_ref[...])
    seq_start = (i % tiles_per_seq) == 0
    buf_ref[0:SUBLANES, :] = jnp.where(seq_start, 0.0, carry_ref[g])
    buf_ref[SUBLANES:SUBLANES + tm, :] = cu
    carry_ref[g] = cu[tm - SUBLANES:tm, :]
    cw = cw_ref[...]
    y = cu * cw[CONV_K - 1:CONV_K, :]
    for tap in range(CONV_K - 1):
        lag = CONV_K - 1 - tap
        y = y + buf_ref[SUBLANES - lag:SUBLANES - lag + tm, :] * cw[tap:tap + 1, :]
    o_ref[...] = (b * y * (z * _sigmoid(z))).astype(o_ref.dtype)


def _norm_conv_branch(x2d, norm_g, w_front, conv_w, seq, tm):
    t, d = x2d.shape
    n_groups = CONV_WIDTH // CONV_GROUP
    kern = functools.partial(_conv_kernel, tm=tm, tiles_per_seq=seq // tm)
    part = lambda k: pl.BlockSpec((CONV_GROUP, d), lambda i, g: (k * n_groups + g, 0))
    return pl.pallas_call(
        kern,
        out_shape=(jax.ShapeDtypeStruct((t, CONV_WIDTH), BF16), jax.ShapeDtypeStruct((t, d), BF16)),
        grid=(t // tm, n_groups),
        in_specs=[pl.BlockSpec((tm, d), lambda i, g: (i, 0)),
                  pl.BlockSpec((1, d), lambda i, g: (0, 0)),
                  part(0), part(1), part(2), part(3),
                  pl.BlockSpec((CONV_K, CONV_GROUP), lambda i, g: (0, g))],
        out_specs=(pl.BlockSpec((tm, CONV_GROUP), lambda i, g: (i, g)),
                   pl.BlockSpec((tm, d), lambda i, g: (i, 0))),
        scratch_shapes=[pltpu.VMEM((n_groups, SUBLANES, CONV_GROUP), F32),
                        pltpu.VMEM((tm + SUBLANES, CONV_GROUP), F32)],
        compiler_params=_params("arbitrary", "arbitrary"),
        name="norm_conv_branch",
    )(x2d, norm_g, w_front, w_front, w_front, w_front, conv_w)


def _rope_t(x, g, cost, sint):
    half = QK_ROPE // 2
    x = x * lax.rsqrt(jnp.sum(x * x, axis=0, keepdims=True) * (1.0 / QK_ROPE) + EPS) * g
    x1 = x[0:half, :]
    x2 = x[half:QK_ROPE, :]
    return x1 * cost - x2 * sint, x2 * cost + x1 * sint


def _mla_prep_kernel(h_ref, win_ref, wkrt_ref, wuqt_ref, wkn_ref, wvt_ref, gq_ref, gkv_ref, gqn_ref,
                     gqr_ref, gkn_ref, gkr_ref, cost_ref, sint_ref, qt_ref, k_ref, vt_ref):
    tm = h_ref.shape[0]
    h = h_ref[...]
    p = _dot_nt(h, win_ref[...])
    cqn = (_rms(p[:, 0:Q_LORA], Q_LORA) * gq_ref[...]).astype(BF16)
    ckvn = (_rms(p[:, Q_LORA:Q_LORA + KV_LORA], KV_LORA) * gkv_ref[...]).astype(BF16)
    qt = _dot_nt(wuqt_ref[...], cqn)
    kn = _dot(ckvn, wkn_ref[...])
    vt = _dot_nt(wvt_ref[...], ckvn)
    krt = _dot_nt(wkrt_ref[...], h)

    cost = cost_ref[...]
    sint = sint_ref[...]
    kr1, kr2 = _rope_t(krt, jnp.broadcast_to(gkr_ref[...], (QK_ROPE, tm)), cost, sint)
    zeros_f = jnp.zeros((HEAD_PAD - QK_NOPE - QK_ROPE, tm), F32)
    k_rope = jnp.concatenate([kr1, kr2, zeros_f], axis=0).T.astype(BF16)

    gqn = jnp.broadcast_to(gqn_ref[...], (QK_NOPE, tm))
    gqr = jnp.broadcast_to(gqr_ref[...], (QK_ROPE, tm))
    gkn = gkn_ref[...]
    zeros = zeros_f.astype(BF16)
    for hd in range(MLA_HEADS):
        lo = hd * (QK_NOPE + QK_ROPE)
        qn = qt[lo:lo + QK_NOPE, :]
        qr = qt[lo + QK_NOPE:lo + QK_NOPE + QK_ROPE, :]
        qn = qn * lax.rsqrt(jnp.sum(qn * qn, axis=0, keepdims=True) * (1.0 / QK_NOPE) + EPS) * gqn
        r1, r2 = _rope_t(qr, gqr, cost, sint)
        half = QK_ROPE // 2
        qt_ref[0, hd, 0:QK_NOPE, :] = qn.astype(BF16)
        qt_ref[0, hd, QK_NOPE:QK_NOPE + half, :] = r1.astype(BF16)
        qt_ref[0, hd, QK_NOPE + half:QK_NOPE + QK_ROPE, :] = r2.astype(BF16)
        qt_ref[0, hd, QK_NOPE + QK_ROPE:HEAD_PAD, :] = zeros
        kh = kn[:, hd * QK_NOPE:(hd + 1) * QK_NOPE]
        k_ref[0, hd, :, 0:QK_NOPE] = (_rms(kh, QK_NOPE) * gkn).astype(BF16)
        k_ref[0, hd, :, QK_NOPE:HEAD_PAD] = k_rope
        vt_ref[0, hd, 0, :, :] = vt[hd * V_HEAD:(hd + 1) * V_HEAD, :].astype(BF16)


def _mla_prep(h, w_front, w_uqt, w_kn, w_vt, gq, gkv, gqn, gqr, gkn, gkr, cost, sint,
              batch, seq, tm, tk):
    t, d = h.shape
    ns = seq // tm
    per_chunk = tk // tm
    lora = Q_LORA + KV_LORA
    const = lambda b, s: (0, 0)
    row = lambda b, s: (b * ns + s, 0)
    col = lambda b, s: (0, b * ns + s)
    qt_shape = jax.ShapeDtypeStruct((batch, MLA_HEADS, HEAD_PAD, seq), BF16)
    k_shape = jax.ShapeDtypeStruct((batch, MLA_HEADS, seq, HEAD_PAD), BF16)
    vt_shape = jax.ShapeDtypeStruct((batch, MLA_HEADS, seq // tk, V_HEAD, tk), BF16)
    full = lambda a: pl.BlockSpec(a.shape, const)
    return pl.pallas_call(
        _mla_prep_kernel,
        out_shape=(qt_shape, k_shape, vt_shape),
        grid=(batch, ns),
        in_specs=[pl.BlockSpec((tm, d), row),
                  pl.BlockSpec((lora, d), lambda b, s: (4 * CONV_WIDTH // lora, 0)),
                  pl.BlockSpec((QK_ROPE, d), lambda b, s: ((4 * CONV_WIDTH + lora) // QK_ROPE, 0)),
                  full(w_uqt), full(w_kn), full(w_vt),
                  full(gq), full(gkv), full(gqn), full(gqr), full(gkn), full(gkr),
                  pl.BlockSpec((QK_ROPE // 2, tm), col), pl.BlockSpec((QK_ROPE // 2, tm), col)],
        out_specs=(pl.BlockSpec((1, MLA_HEADS, HEAD_PAD, tm), lambda b, s: (b, 0, 0, s)),
                   pl.BlockSpec((1, MLA_HEADS, tm, HEAD_PAD), lambda b, s: (b, 0, s, 0)),
                   pl.BlockSpec((1, MLA_HEADS, 1, V_HEAD, tm),
                                lambda b, s: (b, 0, s // per_chunk, 0, s % per_chunk))),
        compiler_params=_params("arbitrary", "arbitrary"),
        name="mla_prep",
    )(h, w_front, w_front, w_uqt, w_kn, w_vt, gq, gkv, gqn, gqr, gkn, gkr, cost, sint)


def _attn_kernel(qt_ref, qtn_ref, k_ref, vt_ref, sz_ref, o_ref, acc_ref, m_ref, s0_ref, s1_ref,
                 *, tq, heads):
    j = pl.program_id(2)
    s_bufs = (s0_ref, s1_ref)

    def scores(hh, ki, q, dst):
        start = pl.multiple_of(ki * tq, tq)
        s = _dot(k_ref[0, hh, pl.ds(start, tq), :], q)
        dst[hh, 0:tq, :] = s
        dst[hh, tq:tq + 1, :] = jnp.max(s, axis=0, keepdims=True)

    def update(hh, ki, src, diagonal):
        s = src[hh, 0:tq, :]
        if diagonal:
            kc = lax.broadcasted_iota(jnp.int32, s.shape, 0) // CHUNK
            qc = lax.broadcasted_iota(jnp.int32, s.shape, 1) // CHUNK
            s = jnp.where(kc <= qc, s, NEG_BIG)
            m_tile = jnp.max(s, axis=0, keepdims=True)
        else:
            m_tile = src[hh, tq:tq + 1, :]
        m = m_ref[hh]
        m_new = jnp.maximum(m, m_tile)
        alpha = jnp.exp2(m - m_new)
        p = jnp.exp2(s - m_new)
        m_ref[hh] = m_new
        vt1 = jnp.concatenate([vt_ref[0, hh, ki], ones_rows], axis=0)
        acc_ref[hh] = alpha * acc_ref[hh] + _dot(vt1, p.astype(BF16))

    ones_rows = (lax.broadcasted_iota(jnp.int32, (BF16_SUBLANES, tq), 0) == 0).astype(BF16)

    @pl.when(j == 0)
    def _():
        for hh in range(heads):
            scores(hh, 0, qt_ref[0, hh, :, 0:tq], s0_ref)

    for u in range(ATTN_GROUP):
        q_tile = ATTN_GROUP * j + u
        first = ((u + 1) // 2) % 2
        bufs = (s_bufs[first], s_bufs[1 - first])

        def q_cur(hh, u=u):
            return qt_ref[0, hh, :, u * tq:(u + 1) * tq]

        def q_next(hh, u=u):
            if u + 1 < ATTN_GROUP:
                return qt_ref[0, hh, :, (u + 1) * tq:(u + 2) * tq]
            return qtn_ref[0, hh]

        acc_ref[...] = jnp.zeros_like(acc_ref)
        m_ref[...] = jnp.full(m_ref.shape, NEG_BIG, F32)

        def step(ki, parity, bufs=bufs, q_cur=q_cur):
            for hh in range(heads):
                scores(hh, ki + 1, q_cur(hh), bufs[1 - parity])
                update(hh, ki, bufs[parity], False)

        def group(jj, carry, step=step):
            for r in range(ATTN_GROUP):
                step(ATTN_GROUP * jj + r, r % 2)
            return carry

        lax.fori_loop(0, j, group, 0)
        for r in range(u):
            step(ATTN_GROUP * j + r, r % 2)
        for hh in range(heads):
            scores(hh, 0, q_next(hh), bufs[1 - u % 2])
            update(hh, q_tile, bufs[u % 2], True)
            l = acc_ref[hh, V_HEAD:V_HEAD + 1, :]
            y = (acc_ref[hh, 0:V_HEAD, :] * (1.0 / l)).T
            sz = sz_ref[0, u * tq:(u + 1) * tq, hh * V_HEAD:(hh + 1) * V_HEAD].astype(F32)
            o_ref[0, u * tq:(u + 1) * tq, hh * V_HEAD:(hh + 1) * V_HEAD] = (y * sz).astype(o_ref.dtype)


def _mla_attention(qt, k, vt, sz, tq, heads):
    batch, n_heads, seq, _ = k.shape
    assert ATTN_GROUP % 4 == 0
    kern = functools.partial(_attn_kernel, tq=tq, heads=heads)
    tg = ATTN_GROUP * tq
    last = seq // tq - 1
    return pl.pallas_call(
        kern,
        out_shape=jax.ShapeDtypeStruct((batch, seq, MLA_WIDTH), BF16),
        grid=(batch, n_heads // heads, seq // tg),
        in_specs=[pl.BlockSpec((1, heads, HEAD_PAD, tg), lambda b, h, i: (b, h, 0, i)),
                  pl.BlockSpec((1, heads, HEAD_PAD, tq),
                               lambda b, h, i: (b, h, 0, jnp.minimum(ATTN_GROUP * (i + 1), last))),
                  pl.BlockSpec((1, heads, seq, HEAD_PAD), lambda b, h, i: (b, h, 0, 0)),
                  pl.BlockSpec((1, heads, seq // tq, V_HEAD, tq), lambda b, h, i: (b, h, 0, 0, 0)),
                  pl.BlockSpec((1, tg, heads * V_HEAD), lambda b, h, i: (b, i, h))],
        out_specs=pl.BlockSpec((1, tg, heads * V_HEAD), lambda b, h, i: (b, i, h)),
        scratch_shapes=[pltpu.VMEM((heads, V_HEAD + BF16_SUBLANES, tq), F32),
                        pltpu.VMEM((heads, 1, tq), F32),
                        pltpu.VMEM((heads, tq + SUBLANES, tq), F32),
                        pltpu.VMEM((heads, tq + SUBLANES, tq), F32)],
        compiler_params=_params("arbitrary", "arbitrary", "arbitrary"),
        name="mla_attention",
    )(qt, qt, k, vt, sz)


def _mem_kv_kernel(mem_ref, g_ref, w_ref, gk_ref, k_ref, v_ref):
    m = mem_ref[0]
    kv = _dot((_rms(m, m.shape[-1]) * g_ref[...]).astype(BF16), w_ref[...])
    gk = gk_ref[...]
    for hd in range(MEM_HEADS):
        lo = hd * MEM_HEAD_DIM
        k_ref[0, :, lo:lo + MEM_HEAD_DIM] = (_rms(kv[:, lo:lo + MEM_HEAD_DIM], MEM_HEAD_DIM) * gk).astype(BF16)
    v_ref[0] = kv[:, MEM_WIDTH:].astype(BF16)


def _mem_kv(mem, g, w, gk):
    batch, m, d = mem.shape
    const = lambda b: (0, 0)
    shape = jax.ShapeDtypeStruct((batch, m, MEM_WIDTH), BF16)
    return pl.pallas_call(
        _mem_kv_kernel,
        out_shape=(shape, shape),
        grid=(batch,),
        in_specs=[pl.BlockSpec((1, m, d), lambda b: (b, 0, 0)),
                  pl.BlockSpec(g.shape, const),
                  pl.BlockSpec(w.shape, const),
                  pl.BlockSpec(gk.shape, const)],
        out_specs=(pl.BlockSpec((1, m, MEM_WIDTH), lambda b: (b, 0, 0)),
                   pl.BlockSpec((1, m, MEM_WIDTH), lambda b: (b, 0, 0))),
        compiler_params=_params("arbitrary"),
        name="mem_kv",
    )(mem, g, w, gk)


def _mem_attn_kernel(h_ref, w_ref, k_ref, v_ref, gq_ref, o_ref):
    p = _dot_nt(h_ref[...], w_ref[...])
    gq = gq_ref[...] * (MEM_HEAD_DIM ** -0.5)
    for hd in range(MEM_HEADS):
        lo = hd * MEM_HEAD_DIM
        q = (_rms(p[:, lo:lo + MEM_HEAD_DIM], MEM_HEAD_DIM) * gq).astype(BF16)
        z = p[:, MEM_WIDTH + lo:MEM_WIDTH + lo + MEM_HEAD_DIM]
        s = _dot_nt(q, k_ref[0, :, lo:lo + MEM_HEAD_DIM])
        e = jnp.exp(s - jnp.max(s, axis=-1, keepdims=True))
        y = _dot(e.astype(BF16), v_ref[0, :, lo:lo + MEM_HEAD_DIM])
        y = y * (1.0 / jnp.sum(e, axis=-1, keepdims=True))
        o_ref[:, lo:lo + MEM_HEAD_DIM] = (y * (z * _sigmoid(z))).astype(o_ref.dtype)


def _mem_attention(h, w, col0, k_mem, v_mem, gq, batch, seq, tm):
    t, d = h.shape
    ns = seq // tm
    m = k_mem.shape[1]
    const = lambda b, s: (0, 0)
    return pl.pallas_call(
        _mem_attn_kernel,
        out_shape=jax.ShapeDtypeStruct((t, MEM_WIDTH), BF16),
        grid=(batch, ns),
        in_specs=[pl.BlockSpec((tm, d), lambda b, s: (b * ns + s, 0)),
                  pl.BlockSpec((pl.Element(2 * MEM_WIDTH), pl.Element(d)), lambda b, s: (col0, 0)),
                  pl.BlockSpec((1, m, MEM_WIDTH), lambda b, s: (b, 0, 0)),
                  pl.BlockSpec((1, m, MEM_WIDTH), lambda b, s: (b, 0, 0)),
                  pl.BlockSpec(gq.shape, const)],
        out_specs=pl.BlockSpec((tm, MEM_WIDTH), lambda b, s: (b * ns + s, 0)),
        compiler_params=_params("arbitrary", "arbitrary"),
        name="mem_attention",
    )(h, w, k_mem, v_mem, gq)


def _gate_kernel(h_ref, w_ref, o_ref, *, silu):
    p = _dot_nt(h_ref[...], w_ref[...])
    s = _sigmoid(p)
    o_ref[...] = ((p * s) if silu else s).astype(o_ref.dtype)


def _gate_proj(h, w, col0, n, silu, tm, tn):
    t, d = h.shape
    assert col0 % BF16_SUBLANES == 0 and n % tn == 0
    return pl.pallas_call(
        functools.partial(_gate_kernel, silu=silu),
        out_shape=jax.ShapeDtypeStruct((t, n), BF16),
        grid=(t // tm, n // tn),
        in_specs=[pl.BlockSpec((tm, d), lambda i, j: (i, 0)),
                  pl.BlockSpec((pl.Element(tn), pl.Element(d)),
                               lambda i, j: (pl.multiple_of(col0 + j * tn, BF16_SUBLANES), 0))],
        out_specs=pl.BlockSpec((tm, tn), lambda i, j: (i, j)),
        compiler_params=_params("arbitrary", "arbitrary"),
        name="silu_gate_proj" if silu else "sigmoid_gate_proj",
    )(h, w)


def _merge_out_kernel(ac_ref, am_ref, ae_ref, gc_ref, gm_ref, ge_ref, x_ref, wc_ref, wm_ref, we_ref,
                      wo_ref, o_ref):
    merged = (gc_ref[...].astype(F32) * _dot(ac_ref[...], wc_ref[...])
              + gm_ref[...].astype(F32) * _dot(am_ref[...], wm_ref[...])
              + ge_ref[...].astype(F32) * _dot(ae_ref[...], we_ref[...]))
    o_ref[...] = x_ref[...] + _dot(merged.astype(BF16), wo_ref[...])


def _merge_out(a_conv, a_mla, a_mem, gates, x2d, w_conv, w_mla, w_mem, w_o, tm):
    t, d = x2d.shape
    row = lambda a: pl.BlockSpec((tm, a.shape[1]), lambda i: (i, 0))
    gate = lambda k: pl.BlockSpec((tm, d), lambda i: (i, k))
    wgt = lambda w: pl.BlockSpec(w.shape, lambda i: (0, 0), pipeline_mode=pl.Buffered(1))
    return pl.pallas_call(
        _merge_out_kernel,
        out_shape=jax.ShapeDtypeStruct((t, d), F32),
        grid=(t // tm,),
        in_specs=[row(a_conv), row(a_mla), row(a_mem), gate(0), gate(1), gate(2), row(x2d),
                  wgt(w_conv), wgt(w_mla), wgt(w_mem), wgt(w_o)],
        out_specs=pl.BlockSpec((tm, d), lambda i: (i, 0)),
        compiler_params=pltpu.CompilerParams(dimension_semantics=("arbitrary",),
                                             vmem_limit_bytes=MERGE_VMEM_LIMIT_BYTES),
        name="merge_out",
    )(a_conv, a_mla, a_mem, gates, gates, gates, x2d, w_conv, w_mla, w_mem, w_o)


def _layer(x, cost, sint, mem, norm_g, w_in, conv_w, w_conv_out, mla_q_norm_g, w_uq, mla_kv_norm_g,
           w_ukv, mla_qn_nope_g, mla_qn_rope_g, mla_kn_nope_g, mla_kn_rope_g, w_mla_out, mem_norm_g,
           w_mem_kv, mem_qn_g, mem_kn_g, w_mem_out, w_o):
    batch, seq, d = x.shape
    t = batch * seq
    x2d = x.reshape(t, d)

    o_rest = 4 * CONV_WIDTH + Q_LORA + KV_LORA + QK_ROPE
    w_front = w_in.T.astype(BF16)
    w_uqt = w_uq.astype(BF16).T
    w_ukv_h = w_ukv.astype(BF16).reshape(KV_LORA, MLA_HEADS, QK_NOPE + V_HEAD)
    w_kn = w_ukv_h[:, :, :QK_NOPE].reshape(KV_LORA, MLA_HEADS * QK_NOPE)
    w_vt = w_ukv_h[:, :, QK_NOPE:].reshape(KV_LORA, MLA_WIDTH).T
    qscale = (QK_NOPE + QK_ROPE) ** -0.5 * LOG2_E

    conv_act, h = _norm_conv_branch(x2d, norm_g[None, :], w_front, conv_w, seq, tm=PROJ_ROW_TILE)
    qt, k, vt = _mla_prep(h, w_front, w_uqt, w_kn, w_vt, mla_q_norm_g[None, :],
                          mla_kv_norm_g[None, :], (mla_qn_nope_g * qscale)[:, None],
                          (mla_qn_rope_g * qscale)[:, None], mla_kn_nope_g[None, :],
                          mla_kn_rope_g[:, None], cost, sint, batch, seq, tm=PREP_ROW_TILE, tk=ATTN_TILE)
    sz_mla = _gate_proj(h, w_front, o_rest, MLA_WIDTH, True, tm=PROJ_ROW_TILE, tn=GATE_COL_TILE)
    gates = _gate_proj(h, w_front, o_rest + MLA_WIDTH + 2 * MEM_WIDTH, 3 * d, False, tm=PROJ_ROW_TILE,
                       tn=GATE_COL_TILE)
    mla_act = _mla_attention(qt, k, vt, sz_mla.reshape(batch, seq, MLA_WIDTH), tq=ATTN_TILE,
                             heads=ATTN_HEADS_PER_STEP)
    k_mem, v_mem = _mem_kv(mem, mem_norm_g[None, :], w_mem_kv.astype(BF16), mem_kn_g[None, :])
    mem_act = _mem_attention(h, w_front, o_rest + MLA_WIDTH, k_mem, v_mem, mem_qn_g[None, :], batch, seq,
                             tm=MEM_ROW_TILE)
    out = _merge_out(conv_act, mla_act.reshape(t, MLA_WIDTH), mem_act, gates, x2d, w_conv_out.astype(BF16),
                     w_mla_out.astype(BF16), w_mem_out.astype(BF16), w_o.astype(BF16), tm=MERGE_ROW_TILE)
    return out.reshape(batch, seq, d)


def kernel(x, positions, mem, norm_g, w_in, conv_w, w_conv_out, mla_q_norm_g, w_uq, mla_kv_norm_g, w_ukv, mla_qn_nope_g, mla_qn_rope_g, mla_kn_nope_g, mla_kn_rope_g, w_mla_out, mem_norm_g, w_mem_kv, mem_qn_g, mem_kn_g, w_mem_out, w_o):
    batch, seq, _ = x.shape
    half = QK_ROPE // 2
    inv_freq = jnp.power(ROPE_THETA, -jnp.arange(half, dtype=F32) / half)
    ang = positions.astype(F32)[..., None] * inv_freq
    cost = jnp.cos(ang).reshape(batch * seq, half).T
    sint = jnp.sin(ang).reshape(batch * seq, half).T
    for l in range(norm_g.shape[0]):
        x = _layer(x, cost, sint, mem, norm_g[l], w_in[l], conv_w[l], w_conv_out[l], mla_q_norm_g[l],
                   w_uq[l], mla_kv_norm_g[l], w_ukv[l], mla_qn_nope_g[l], mla_qn_rope_g[l],
                   mla_kn_nope_g[l], mla_kn_rope_g[l], w_mla_out[l], mem_norm_g[l], w_mem_kv[l],
                   mem_qn_g[l], mem_kn_g[l], w_mem_out[l], w_o[l])
    return x
```

```python
import functools

import jax
import jax.numpy as jnp
from jax import lax
from jax.experimental import pallas as pl
from jax.experimental.pallas import tpu as pltpu

F32 = jnp.float32
BF16 = jnp.bfloat16

EPS = 1e-6
CHUNK = 64
CONV_WIDTH = 1024
CONV_K = 3
CONV_GROUP = 256
MLA_HEADS = 16
Q_LORA = 512
KV_LORA = 512
QK_NOPE = 128
QK_ROPE = 64
V_HEAD = 128
MLA_WIDTH = MLA_HEADS * V_HEAD
HEAD_PAD = 256
ROPE_THETA = 10000.0
MEM_HEADS = 4
MEM_HEAD_DIM = 256
MEM_WIDTH = MEM_HEADS * MEM_HEAD_DIM
NEG_BIG = -1e30
LOG2_E = 1.4426950408889634
ATTN_TILE = 512
ATTN_GROUP = 4
ATTN_HEADS_PER_STEP = 2

PROJ_ROW_TILE = 1024
GATE_COL_TILE = 2048
MEM_ROW_TILE = 1024
PREP_ROW_TILE = 256
MERGE_ROW_TILE = 256

SUBLANES = 8
BF16_SUBLANES = 16
VMEM_LIMIT_BYTES = 52 * 1024 * 1024
MERGE_VMEM_LIMIT_BYTES = 50 * 1024 * 1024


def _params(*sem):
    return pltpu.CompilerParams(dimension_semantics=sem, vmem_limit_bytes=VMEM_LIMIT_BYTES)


def _rms(x, width):
    ms = jnp.sum(x * x, axis=-1, keepdims=True) * (1.0 / width)
    return x * lax.rsqrt(ms + EPS)


def _sigmoid(x):
    return 0.5 * jnp.tanh(0.5 * x) + 0.5


def _dot(a, b):
    return jnp.dot(a, b, preferred_element_type=F32)


def _dot_nt(a, b):
    return lax.dot_general(a, b, (((1,), (1,)), ((), ())), preferred_element_type=F32)


def _conv_kernel(x_ref, ng_ref, wc_ref, wb_ref, wu_ref, wz_ref, cw_ref, wf_ref, o_ref, h_ref, wr_ref,
                 carry_ref, buf_ref, *, tm, tiles_per_seq):
    i = pl.program_id(0)
    g = pl.program_id(1)
    wr_ref[...] = wf_ref[...].astype(BF16)

    @pl.when(g == 0)
    def _():
        x = x_ref[...]
        h_ref[...] = (_rms(x, x.shape[-1]) * ng_ref[...]).astype(BF16)

    h = h_ref[...]
    b = _dot_nt(h, wb_ref[...])
    z = _dot_nt(h, wz_ref[...])
    cu = _dot_nt(h, wc_ref[...]) * _dot_nt(h, wu_ref[...])
    seq_start = (i % tiles_per_seq) == 0
    buf_ref[0:SUBLANES, :] = jnp.where(seq_start, 0.0, carry_ref[g])
    buf_ref[SUBLANES:SUBLANES + tm, :] = cu
    carry_ref[g] = cu[tm - SUBLANES:tm, :]
    cw = cw_ref[...]
    y = cu * cw[CONV_K - 1:CONV_K, :]
    for tap in range(CONV_K - 1):
        lag = CONV_K - 1 - tap
        y = y + buf_ref[SUBLANES - lag:SUBLANES - lag + tm, :] * cw[tap:tap + 1, :]
    o_ref[...] = (b * y * (z * _sigmoid(z))).astype(o_ref.dtype)


def _norm_conv_branch(x2d, norm_g, w_front, conv_w, w_in_t, row0, seq, tm):
    t, d = x2d.shape
    n_groups = CONV_WIDTH // CONV_GROUP
    n_steps = (t // tm) * n_groups
    rest = w_in_t.shape[0] - row0
    slab = rest // n_steps
    assert slab * n_steps == rest and slab % BF16_SUBLANES == 0 and row0 % SUBLANES == 0
    kern = functools.partial(_conv_kernel, tm=tm, tiles_per_seq=seq // tm)
    part = lambda k: pl.BlockSpec((CONV_GROUP, d), lambda i, g: (k * n_groups + g, 0))
    return pl.pallas_call(
        kern,
        out_shape=(jax.ShapeDtypeStruct((t, CONV_WIDTH), BF16), jax.ShapeDtypeStruct((t, d), BF16),
                   jax.ShapeDtypeStruct((rest, d), BF16)),
        grid=(t // tm, n_groups),
        in_specs=[pl.BlockSpec((tm, d), lambda i, g: (i, 0)),
                  pl.BlockSpec((1, d), lambda i, g: (0, 0)),
                  part(0), part(1), part(2), part(3),
                  pl.BlockSpec((CONV_K, CONV_GROUP), lambda i, g: (0, g)),
                  pl.BlockSpec((pl.Element(slab), pl.Element(d)),
                               lambda i, g: (pl.multiple_of(row0 + (i * n_groups + g) * slab, SUBLANES), 0))],
        out_specs=(pl.BlockSpec((tm, CONV_GROUP), lambda i, g: (i, g)),
                   pl.BlockSpec((tm, d), lambda i, g: (i, 0)),
                   pl.BlockSpec((slab, d), lambda i, g: (i * n_groups + g, 0))),
        scratch_shapes=[pltpu.VMEM((n_groups, SUBLANES, CONV_GROUP), F32),
                        pltpu.VMEM((tm + SUBLANES, CONV_GROUP), F32)],
        compiler_params=_params("arbitrary", "arbitrary"),
        name="norm_conv_branch",
    )(x2d, norm_g, w_front, w_front, w_front, w_front, conv_w, w_in_t)


def _rope_t(x, g, cost, sint):
    half = QK_ROPE // 2
    x = x * lax.rsqrt(jnp.sum(x * x, axis=0, keepdims=True) * (1.0 / QK_ROPE) + EPS) * g
    x1 = x[0:half, :]
    x2 = x[half:QK_ROPE, :]
    return x1 * cost - x2 * sint, x2 * cost + x1 * sint


def _mla_prep_kernel(h_ref, win_ref, wkrt_ref, wuqt_ref, wkn_ref, wvt_ref, gq_ref, gkv_ref, gqn_ref,
                     gqr_ref, gkn_ref, gkr_ref, cost_ref, sint_ref, qt_ref, k_ref, vt_ref):
    tm = h_ref.shape[0]
    h = h_ref[...]
    p = _dot_nt(h, win_ref[...])
    cqn = (_rms(p[:, 0:Q_LORA], Q_LORA) * gq_ref[...]).astype(BF16)
    ckvn = (_rms(p[:, Q_LORA:Q_LORA + KV_LORA], KV_LORA) * gkv_ref[...]).astype(BF16)
    qt = _dot_nt(wuqt_ref[...], cqn)
    kn = _dot(ckvn, wkn_ref[...])
    vt = _dot_nt(wvt_ref[...], ckvn)
    krt = _dot_nt(wkrt_ref[...], h)

    cost = cost_ref[...]
    sint = sint_ref[...]
    kr1, kr2 = _rope_t(krt, jnp.broadcast_to(gkr_ref[...], (QK_ROPE, tm)), cost, sint)
    zeros_f = jnp.zeros((HEAD_PAD - QK_NOPE - QK_ROPE, tm), F32)
    k_rope = jnp.concatenate([kr1, kr2, zeros_f], axis=0).T.astype(BF16)

    gqn = jnp.broadcast_to(gqn_ref[...], (QK_NOPE, tm))
    gqr = jnp.broadcast_to(gqr_ref[...], (QK_ROPE, tm))
    gkn = gkn_ref[...]
    zeros = zeros_f.astype(BF16)
    for hd in range(MLA_HEADS):
        lo = hd * (QK_NOPE + QK_ROPE)
        qn = qt[lo:lo + QK_NOPE, :]
        qr = qt[lo + QK_NOPE:lo + QK_NOPE + QK_ROPE, :]
        qn = qn * lax.rsqrt(jnp.sum(qn * qn, axis=0, keepdims=True) * (1.0 / QK_NOPE) + EPS) * gqn
        r1, r2 = _rope_t(qr, gqr, cost, sint)
        half = QK_ROPE // 2
        qt_ref[0, hd, 0:QK_NOPE, :] = qn.astype(BF16)
        qt_ref[0, hd, QK_NOPE:QK_NOPE + half, :] = r1.astype(BF16)
        qt_ref[0, hd, QK_NOPE + half:QK_NOPE + QK_ROPE, :] = r2.astype(BF16)
        qt_ref[0, hd, QK_NOPE + QK_ROPE:HEAD_PAD, :] = zeros
        kh = kn[:, hd * QK_NOPE:(hd + 1) * QK_NOPE]
        k_ref[0, hd, :, 0:QK_NOPE] = (_rms(kh, QK_NOPE) * gkn).astype(BF16)
        k_ref[0, hd, :, QK_NOPE:HEAD_PAD] = k_rope
        vt_ref[0, hd, 0, :, :] = vt[hd * V_HEAD:(hd + 1) * V_HEAD, :].astype(BF16)


def _mla_prep(h, w_front, w_uqt, w_kn, w_vt, gq, gkv, gqn, gqr, gkn, gkr, cost, sint,
              batch, seq, tm, tk):
    t, d = h.shape
    ns = seq // tm
    per_chunk = tk // tm
    lora = Q_LORA + KV_LORA
    const = lambda b, s: (0, 0)
    row = lambda b, s: (b * ns + s, 0)
    col = lambda b, s: (0, b * ns + s)
    qt_shape = jax.ShapeDtypeStruct((batch, MLA_HEADS, HEAD_PAD, seq), BF16)
    k_shape = jax.ShapeDtypeStruct((batch, MLA_HEADS, seq, HEAD_PAD), BF16)
    vt_shape = jax.ShapeDtypeStruct((batch, MLA_HEADS, seq // tk, V_HEAD, tk), BF16)
    full = lambda a: pl.BlockSpec(a.shape, const)
    return pl.pallas_call(
        _mla_prep_kernel,
        out_shape=(qt_shape, k_shape, vt_shape),
        grid=(batch, ns),
        in_specs=[pl.BlockSpec((tm, d), row),
                  pl.BlockSpec((lora, d), lambda b, s: (4 * CONV_WIDTH // lora, 0)),
                  pl.BlockSpec((QK_ROPE, d), lambda b, s: ((4 * CONV_WIDTH + lora) // QK_ROPE, 0)),
                  full(w_uqt), full(w_kn), full(w_vt),
                  full(gq), full(gkv), full(gqn), full(gqr), full(gkn), full(gkr),
                  pl.BlockSpec((QK_ROPE // 2, tm), col), pl.BlockSpec((QK_ROPE // 2, tm), col)],
        out_specs=(pl.BlockSpec((1, MLA_HEADS, HEAD_PAD, tm), lambda b, s: (b, 0, 0, s)),
                   pl.BlockSpec((1, MLA_HEADS, tm, HEAD_PAD), lambda b, s: (b, 0, s, 0)),
                   pl.BlockSpec((1, MLA_HEADS, 1, V_HEAD, tm),
                                lambda b, s: (b, 0, s // per_chunk, 0, s % per_chunk))),
        compiler_params=_params("arbitrary", "arbitrary"),
        name="mla_prep",
    )(h, w_front, w_front, w_uqt, w_kn, w_vt, gq, gkv, gqn, gqr, gkn, gkr, cost, sint)


def _attn_kernel(qt_ref, qtn_ref, k_ref, vt_ref, sz_ref, o_ref, acc_ref, m_ref, s0_ref, s1_ref,
                 *, tq, heads):
    j = pl.program_id(2)
    s_bufs = (s0_ref, s1_ref)

    def scores(hh, ki, q, dst):
        start = pl.multiple_of(ki * tq, tq)
        s = _dot(k_ref[0, hh, pl.ds(start, tq), :], q)
        dst[hh, 0:tq, :] = s
        dst[hh, tq:tq + 1, :] = jnp.max(s, axis=0, keepdims=True)

    def update(hh, ki, src, diagonal):
        s = src[hh, 0:tq, :]
        if diagonal:
            kc = lax.broadcasted_iota(jnp.int32, s.shape, 0) // CHUNK
            qc = lax.broadcasted_iota(jnp.int32, s.shape, 1) // CHUNK
            s = jnp.where(kc <= qc, s, NEG_BIG)
            m_tile = jnp.max(s, axis=0, keepdims=True)
        else:
            m_tile = src[hh, tq:tq + 1, :]
        m = m_ref[hh]
        m_new = jnp.maximum(m, m_tile)
        alpha = jnp.exp2(m - m_new)
        p = jnp.exp2(s - m_new)
        m_ref[hh] = m_new
        vt1 = jnp.concatenate([vt_ref[0, hh, ki], ones_rows], axis=0)
        acc_ref[hh] = alpha * acc_ref[hh] + _dot(vt1, p.astype(BF16))

    ones_rows = (lax.broadcasted_iota(jnp.int32, (BF16_SUBLANES, tq), 0) == 0).astype(BF16)

    @pl.when(j == 0)
    def _():
        for hh in range(heads):
            scores(hh, 0, qt_ref[0, hh, :, 0:tq], s0_ref)

    for u in range(ATTN_GROUP):
        q_tile = ATTN_GROUP * j + u
        first = ((u + 1) // 2) % 2
        bufs = (s_bufs[first], s_bufs[1 - first])

        def q_cur(hh, u=u):
            return qt_ref[0, hh, :, u * tq:(u + 1) * tq]

        def q_next(hh, u=u):
            if u + 1 < ATTN_GROUP:
                return qt_ref[0, hh, :, (u + 1) * tq:(u + 2) * tq]
            return qtn_ref[0, hh]

        acc_ref[...] = jnp.zeros_like(acc_ref)
        m_ref[...] = jnp.full(m_ref.shape, NEG_BIG, F32)

        def step(ki, parity, bufs=bufs, q_cur=q_cur):
            for hh in range(heads):
                scores(hh, ki + 1, q_cur(hh), bufs[1 - parity])
                update(hh, ki, bufs[parity], False)

        def group(jj, carry, step=step):
            for r in range(ATTN_GROUP):
                step(ATTN_GROUP * jj + r, r % 2)
            return carry

        lax.fori_loop(0, j, group, 0)
        for r in range(u):
            step(ATTN_GROUP * j + r, r % 2)
        for hh in range(heads):
            scores(hh, 0, q_next(hh), bufs[1 - u % 2])
            update(hh, q_tile, bufs[u % 2], True)
            l = acc_ref[hh, V_HEAD:V_HEAD + 1, :]
            y = (acc_ref[hh, 0:V_HEAD, :] * (1.0 / l)).T
            sz = sz_ref[0, u * tq:(u + 1) * tq, hh * V_HEAD:(hh + 1) * V_HEAD].astype(F32)
            o_ref[0, u * tq:(u + 1) * tq, hh * V_HEAD:(hh + 1) * V_HEAD] = (y * sz).astype(o_ref.dtype)


def _mla_attention(qt, k, vt, sz, tq, heads):
    batch, n_heads, seq, _ = k.shape
    assert ATTN_GROUP % 4 == 0
    kern = functools.partial(_attn_kernel, tq=tq, heads=heads)
    tg = ATTN_GROUP * tq
    last = seq // tq - 1
    return pl.pallas_call(
        kern,
        out_shape=jax.ShapeDtypeStruct((batch, seq, MLA_WIDTH), BF16),
        grid=(batch, n_heads // heads, seq // tg),
        in_specs=[pl.BlockSpec((1, heads, HEAD_PAD, tg), lambda b, h, i: (b, h, 0, i)),
                  pl.BlockSpec((1, heads, HEAD_PAD, tq),
                               lambda b, h, i: (b, h, 0, jnp.minimum(ATTN_GROUP * (i + 1), last))),
                  pl.BlockSpec((1, heads, seq, HEAD_PAD), lambda b, h, i: (b, h, 0, 0)),
                  pl.BlockSpec((1, heads, seq // tq, V_HEAD, tq), lambda b, h, i: (b, h, 0, 0, 0)),
                  pl.BlockSpec((1, tg, heads * V_HEAD), lambda b, h, i: (b, i, h))],
        out_specs=pl.BlockSpec((1, tg, heads * V_HEAD), lambda b, h, i: (b, i, h)),
        scratch_shapes=[pltpu.VMEM((heads, V_HEAD + BF16_SUBLANES, tq), F32),
                        pltpu.VMEM((heads, 1, tq), F32),
                        pltpu.VMEM((heads, tq + SUBLANES, tq), F32),
                        pltpu.VMEM((heads, tq + SUBLANES, tq), F32)],
        compiler_params=_params("arbitrary", "arbitrary", "arbitrary"),
        name="mla_attention",
    )(qt, qt, k, vt, sz)


def _mem_kv_kernel(mem_ref, g_ref, w_ref, gk_ref, k_ref, v_ref):
    m = mem_ref[0]
    kv = _dot((_rms(m, m.shape[-1]) * g_ref[...]).astype(BF16), w_ref[...])
    gk = gk_ref[...]
    for hd in range(MEM_HEADS):
        lo = hd * MEM_HEAD_DIM
        k_ref[0, :, lo:lo + MEM_HEAD_DIM] = (_rms(kv[:, lo:lo + MEM_HEAD_DIM], MEM_HEAD_DIM) * gk).astype(BF16)
    v_ref[0] = kv[:, MEM_WIDTH:].astype(BF16)


def _mem_kv(mem, g, w, gk):
    batch, m, d = mem.shape
    const = lambda b: (0, 0)
    shape = jax.ShapeDtypeStruct((batch, m, MEM_WIDTH), BF16)
    return pl.pallas_call(
        _mem_kv_kernel,
        out_shape=(shape, shape),
        grid=(batch,),
        in_specs=[pl.BlockSpec((1, m, d), lambda b: (b, 0, 0)),
                  pl.BlockSpec(g.shape, const),
                  pl.BlockSpec(w.shape, const),
                  pl.BlockSpec(gk.shape, const)],
        out_specs=(pl.BlockSpec((1, m, MEM_WIDTH), lambda b: (b, 0, 0)),
                   pl.BlockSpec((1, m, MEM_WIDTH), lambda b: (b, 0, 0))),
        compiler_params=_params("arbitrary"),
        name="mem_kv",
    )(mem, g, w, gk)


def _mem_attn_kernel(h_ref, w_ref, k_ref, v_ref, gq_ref, o_ref):
    p = _dot_nt(h_ref[...], w_ref[...])
    gq = gq_ref[...] * (MEM_HEAD_DIM ** -0.5)
    for hd in range(MEM_HEADS):
        lo = hd * MEM_HEAD_DIM
        q = (_rms(p[:, lo:lo + MEM_HEAD_DIM], MEM_HEAD_DIM) * gq).astype(BF16)
        z = p[:, MEM_WIDTH + lo:MEM_WIDTH + lo + MEM_HEAD_DIM]
        s = _dot_nt(q, k_ref[0, :, lo:lo + MEM_HEAD_DIM])
        e = jnp.exp(s - jnp.max(s, axis=-1, keepdims=True))
        y = _dot(e.astype(BF16), v_ref[0, :, lo:lo + MEM_HEAD_DIM])
        y = y * (1.0 / jnp.sum(e, axis=-1, keepdims=True))
        o_ref[:, lo:lo + MEM_HEAD_DIM] = (y * (z * _sigmoid(z))).astype(o_ref.dtype)


def _mem_attention(h, w, col0, k_mem, v_mem, gq, batch, seq, tm):
    t, d = h.shape
    ns = seq // tm
    m = k_mem.shape[1]
    const = lambda b, s: (0, 0)
    return pl.pallas_call(
        _mem_attn_kernel,
        out_shape=jax.ShapeDtypeStruct((t, MEM_WIDTH), BF16),
        grid=(batch, ns),
        in_specs=[pl.BlockSpec((tm, d), lambda b, s: (b * ns + s, 0)),
                  pl.BlockSpec((pl.Element(2 * MEM_WIDTH), pl.Element(d)), lambda b, s: (col0, 0)),
                  pl.BlockSpec((1, m, MEM_WIDTH), lambda b, s: (b, 0, 0)),
                  pl.BlockSpec((1, m, MEM_WIDTH), lambda b, s: (b, 0, 0)),
                  pl.BlockSpec(gq.shape, const)],
        out_specs=pl.BlockSpec((tm, MEM_WIDTH), lambda b, s: (b * ns + s, 0)),
        compiler_params=_params("arbitrary", "arbitrary"),
        name="mem_attention",
    )(h, w, k_mem, v_mem, gq)


def _gate_kernel(h_ref, w_ref, o_ref, *, silu):
    p = _dot_nt(h_ref[...], w_ref[...])
    s = _sigmoid(p)
    o_ref[...] = ((p * s) if silu else s).astype(o_ref.dtype)


def _gate_proj(h, w, col0, n, silu, tm, tn):
    t, d = h.shape
    assert col0 % BF16_SUBLANES == 0 and n % tn == 0
    return pl.pallas_call(
        functools.partial(_gate_kernel, silu=silu),
        out_shape=jax.ShapeDtypeStruct((t, n), BF16),
        grid=(t // tm, n // tn),
        in_specs=[pl.BlockSpec((tm, d), lambda i, j: (i, 0)),
                  pl.BlockSpec((pl.Element(tn), pl.Element(d)),
                               lambda i, j: (pl.multiple_of(col0 + j * tn, BF16_SUBLANES), 0))],
        out_specs=pl.BlockSpec((tm, tn), lambda i, j: (i, j)),
        compiler_params=_params("arbitrary", "arbitrary"),
        name="silu_gate_proj" if silu else "sigmoid_gate_proj",
    )(h, w)


def _merge_out_kernel(ac_ref, am_ref, ae_ref, gc_ref, gm_ref, ge_ref, x_ref, wc_ref, wm_ref, we_ref,
                      wo_ref, o_ref):
    merged = (gc_ref[...].astype(F32) * _dot(ac_ref[...], wc_ref[...])
              + gm_ref[...].astype(F32) * _dot(am_ref[...], wm_ref[...])
              + ge_ref[...].astype(F32) * _dot(ae_ref[...], we_ref[...]))
    o_ref[...] = x_ref[...] + _dot(merged.astype(BF16), wo_ref[...])


def _merge_out(a_conv, a_mla, a_mem, gates, x2d, w_conv, w_mla, w_mem, w_o, tm):
    t, d = x2d.shape
    row = lambda a: pl.BlockSpec((tm, a.shape[1]), lambda i: (i, 0))
    gate = lambda k: pl.BlockSpec((tm, d), lambda i: (i, k))
    wgt = lambda w: pl.BlockSpec(w.shape, lambda i: (0, 0), pipeline_mode=pl.Buffered(1))
    return pl.pallas_call(
        _merge_out_kernel,
        out_shape=jax.ShapeDtypeStruct((t, d), F32),
        grid=(t // tm,),
        in_specs=[row(a_conv), row(a_mla), row(a_mem), gate(0), gate(1), gate(2), row(x2d),
                  wgt(w_conv), wgt(w_mla), wgt(w_mem), wgt(w_o)],
        out_specs=pl.BlockSpec((tm, d), lambda i: (i, 0)),
        compiler_params=pltpu.CompilerParams(dimension_semantics=("arbitrary",),
                                             vmem_limit_bytes=MERGE_VMEM_LIMIT_BYTES),
        name="merge_out",
    )(a_conv, a_mla, a_mem, gates, gates, gates, x2d, w_conv, w_mla, w_mem, w_o)


def _layer(x, cost, sint, mem, norm_g, w_in, conv_w, w_conv_out, mla_q_norm_g, w_uq, mla_kv_norm_g,
           w_ukv, mla_qn_nope_g, mla_qn_rope_g, mla_kn_nope_g, mla_kn_rope_g, w_mla_out, mem_norm_g,
           w_mem_kv, mem_qn_g, mem_kn_g, w_mem_out, w_o):
    batch, seq, d = x.shape
    t = batch * seq
    x2d = x.reshape(t, d)

    o_rest = 4 * CONV_WIDTH + Q_LORA + KV_LORA + QK_ROPE
    w_in_t = w_in.T
    w_front = w_in_t[:o_rest].astype(BF16)
    w_uqt = w_uq.astype(BF16).T
    w_ukv_h = w_ukv.astype(BF16).reshape(KV_LORA, MLA_HEADS, QK_NOPE + V_HEAD)
    w_kn = w_ukv_h[:, :, :QK_NOPE].reshape(KV_LORA, MLA_HEADS * QK_NOPE)
    w_vt = w_ukv_h[:, :, QK_NOPE:].reshape(KV_LORA, MLA_WIDTH).T
    qscale = (QK_NOPE + QK_ROPE) ** -0.5 * LOG2_E

    conv_act, h, w_rest = _norm_conv_branch(x2d, norm_g[None, :], w_front, conv_w, w_in_t, o_rest, seq,
                                            tm=PROJ_ROW_TILE)
    qt, k, vt = _mla_prep(h, w_front, w_uqt, w_kn, w_vt, mla_q_norm_g[None, :],
                          mla_kv_norm_g[None, :], (mla_qn_nope_g * qscale)[:, None],
                          (mla_qn_rope_g * qscale)[:, None], mla_kn_nope_g[None, :],
                          mla_kn_rope_g[:, None], cost, sint, batch, seq, tm=PREP_ROW_TILE, tk=ATTN_TILE)
    sz_mla = _gate_proj(h, w_rest, 0, MLA_WIDTH, True, tm=PROJ_ROW_TILE, tn=GATE_COL_TILE)
    gates = _gate_proj(h, w_rest, MLA_WIDTH + 2 * MEM_WIDTH, 3 * d, False, tm=PROJ_ROW_TILE,
                       tn=GATE_COL_TILE)
    mla_act = _mla_attention(qt, k, vt, sz_mla.reshape(batch, seq, MLA_WIDTH), tq=ATTN_TILE,
                             heads=ATTN_HEADS_PER_STEP)
    k_mem, v_mem = _mem_kv(mem, mem_norm_g[None, :], w_mem_kv.astype(BF16), mem_kn_g[None, :])
    mem_act = _mem_attention(h, w_rest, MLA_WIDTH, k_mem, v_mem, mem_qn_g[None, :], batch, seq,
                             tm=MEM_ROW_TILE)
    out = _merge_out(conv_act, mla_act.reshape(t, MLA_WIDTH), mem_act, gates, x2d, w_conv_out.astype(BF16),
                     w_mla_out.astype(BF16), w_mem_out.astype(BF16), w_o.astype(BF16), tm=MERGE_ROW_TILE)
    return out.reshape(batch, seq, d)


def kernel(x, positions, mem, norm_g, w_in, conv_w, w_conv_out, mla_q_norm_g, w_uq, mla_kv_norm_g, w_ukv, mla_qn_nope_g, mla_qn_rope_g, mla_kn_nope_g, mla_kn_rope_g, w_mla_out, mem_norm_g, w_mem_kv, mem_qn_g, mem_kn_g, w_mem_out, w_o):
    batch, seq, _ = x.shape
    half = QK_ROPE // 2
    inv_freq = jnp.power(ROPE_THETA, -jnp.arange(half, dtype=F32) / half)
    ang = positions.astype(F32)[..., None] * inv_freq
    cost = jnp.cos(ang).reshape(batch * seq, half).T
    sint = jnp.sin(ang).reshape(batch * seq, half).T
    for l in range(norm_g.shape[0]):
        x = _layer(x, cost, sint, mem, norm_g[l], w_in[l], conv_w[l], w_conv_out[l], mla_q_norm_g[l],
                   w_uq[l], mla_kv_norm_g[l], w_ukv[l], mla_qn_nope_g[l], mla_qn_rope_g[l],
                   mla_kn_nope_g[l], mla_kn_rope_g[l], w_mla_out[l], mem_norm_g[l], w_mem_kv[l],
                   mem_qn_g[l], mem_kn_g[l], w_mem_out[l], w_o[l])
    return x
```

```python
import functools

import jax
import jax.numpy as jnp
from jax import lax
from jax.experimental import pallas as pl
from jax.experimental.pallas import tpu as pltpu

F32 = jnp.float32
BF16 = jnp.bfloat16

EPS = 1e-6
CHUNK = 64
CONV_WIDTH = 1024
CONV_K = 3
CONV_GROUP = 256
MLA_HEADS = 16
Q_LORA = 512
KV_LORA = 512
QK_NOPE = 128
QK_ROPE = 64
V_HEAD = 128
MLA_WIDTH = MLA_HEADS * V_HEAD
HEAD_PAD = 256
ROPE_THETA = 10000.0
MEM_HEADS = 4
MEM_HEAD_DIM = 256
MEM_WIDTH = MEM_HEADS * MEM_HEAD_DIM
NEG_BIG = -1e30
LOG2_E = 1.4426950408889634
ATTN_TILE = 512
ATTN_GROUP = 4
ATTN_HEADS_PER_STEP = 2

PROJ_ROW_TILE = 1024
GATE_COL_TILE = 2048
MEM_ROW_TILE = 1024
PREP_ROW_TILE = 256
MERGE_ROW_TILE = 256

SUBLANES = 8
BF16_SUBLANES = 16
VMEM_LIMIT_BYTES = 52 * 1024 * 1024
MERGE_VMEM_LIMIT_BYTES = 50 * 1024 * 1024


def _params(*sem):
    return pltpu.CompilerParams(dimension_semantics=sem, vmem_limit_bytes=VMEM_LIMIT_BYTES)


def _rms(x, width):
    ms = jnp.sum(x * x, axis=-1, keepdims=True) * (1.0 / width)
    return x * lax.rsqrt(ms + EPS)


def _sigmoid(x):
    return 0.5 * jnp.tanh(0.5 * x) + 0.5


def _dot(a, b):
    return jnp.dot(a, b, preferred_element_type=F32)


def _dot_nt(a, b):
    return lax.dot_general(a, b, (((1,), (1,)), ((), ())), preferred_element_type=F32)


def _conv_kernel(x_ref, ng_ref, wc_ref, wb_ref, wu_ref, wz_ref, cw_ref, wf_ref, o_ref, h_ref, wr_ref,
                 carry_ref, buf_ref, *, tm, tiles_per_seq):
    i = pl.program_id(0)
    g = pl.program_id(1)
    wr_ref[...] = wf_ref[...].astype(BF16)

    @pl.when(g == 0)
    def _():
        x = x_ref[...]
        h_ref[...] = (_rms(x, x.shape[-1]) * ng_ref[...]).astype(BF16)

    h = h_ref[...]
    b = _dot_nt(h, wb_ref[...])
    z = _dot_nt(h, wz_ref[...])
    cu = _dot_nt(h, wc_ref[...]) * _dot_nt(h, wu_ref[...])
    seq_start = (i % tiles_per_seq) == 0
    buf_ref[0:SUBLANES, :] = jnp.where(seq_start, 0.0, carry_ref[g])
    buf_ref[SUBLANES:SUBLANES + tm, :] = cu
    carry_ref[g] = cu[tm - SUBLANES:tm, :]
    cw = cw_ref[...]
    y = cu * cw[CONV_K - 1:CONV_K, :]
    for tap in range(CONV_K - 1):
        lag = CONV_K - 1 - tap
        y = y + buf_ref[SUBLANES - lag:SUBLANES - lag + tm, :] * cw[tap:tap + 1, :]
    o_ref[...] = (b * y * (z * _sigmoid(z))).astype(o_ref.dtype)


def _norm_conv_branch(x2d, norm_g, w_front, conv_w, w_in_t, row0, seq, tm):
    t, d = x2d.shape
    n_groups = CONV_WIDTH // CONV_GROUP
    n_steps = (t // tm) * n_groups
    rest = w_in_t.shape[0] - row0
    slab = rest // n_steps
    assert slab * n_steps == rest and slab % BF16_SUBLANES == 0 and row0 % SUBLANES == 0
    kern = functools.partial(_conv_kernel, tm=tm, tiles_per_seq=seq // tm)
    part = lambda k: pl.BlockSpec((CONV_GROUP, d), lambda i, g: (k * n_groups + g, 0))
    return pl.pallas_call(
        kern,
        out_shape=(jax.ShapeDtypeStruct((t, CONV_WIDTH), BF16), jax.ShapeDtypeStruct((t, d), BF16),
                   jax.ShapeDtypeStruct((rest, d), BF16)),
        grid=(t // tm, n_groups),
        in_specs=[pl.BlockSpec((tm, d), lambda i, g: (i, 0)),
                  pl.BlockSpec((1, d), lambda i, g: (0, 0)),
                  part(0), part(1), part(2), part(3),
                  pl.BlockSpec((CONV_K, CONV_GROUP), lambda i, g: (0, g)),
                  pl.BlockSpec((pl.Element(slab), pl.Element(d)),
                               lambda i, g: (pl.multiple_of(row0 + (i * n_groups + g) * slab, SUBLANES), 0))],
        out_specs=(pl.BlockSpec((tm, CONV_GROUP), lambda i, g: (i, g)),
                   pl.BlockSpec((tm, d), lambda i, g: (i, 0)),
                   pl.BlockSpec((slab, d), lambda i, g: (i * n_groups + g, 0))),
        scratch_shapes=[pltpu.VMEM((n_groups, SUBLANES, CONV_GROUP), F32),
                        pltpu.VMEM((tm + SUBLANES, CONV_GROUP), F32)],
        compiler_params=_params("arbitrary", "arbitrary"),
        name="norm_conv_branch",
    )(x2d, norm_g, w_front, w_front, w_front, w_front, conv_w, w_in_t)


def _rope_t(x, g, cost, sint):
    half = QK_ROPE // 2
    x = x * lax.rsqrt(jnp.sum(x * x, axis=0, keepdims=True) * (1.0 / QK_ROPE) + EPS) * g
    x1 = x[0:half, :]
    x2 = x[half:QK_ROPE, :]
    return x1 * cost - x2 * sint, x2 * cost + x1 * sint


def _mla_prep_kernel(h_ref, win_ref, wkrt_ref, wuqt_ref, wkn_ref, wvt_ref, gq_ref, gkv_ref, gqn_ref,
                     gqr_ref, gkn_ref, gkr_ref, cost_ref, sint_ref, qt_ref, k_ref, vt_ref):
    tm = h_ref.shape[0]
    h = h_ref[...]
    p = _dot_nt(h, win_ref[...])
    cqn = (_rms(p[:, 0:Q_LORA], Q_LORA) * gq_ref[...]).astype(BF16)
    ckvn = (_rms(p[:, Q_LORA:Q_LORA + KV_LORA], KV_LORA) * gkv_ref[...]).astype(BF16)
    qt = _dot_nt(wuqt_ref[...], cqn)
    kn = _dot(ckvn, wkn_ref[...])
    vt = _dot_nt(wvt_ref[...], ckvn)
    krt = _dot_nt(wkrt_ref[...], h)

    cost = cost_ref[...]
    sint = sint_ref[...]
    kr1, kr2 = _rope_t(krt, jnp.broadcast_to(gkr_ref[...], (QK_ROPE, tm)), cost, sint)
    zeros_f = jnp.zeros((HEAD_PAD - QK_NOPE - QK_ROPE, tm), F32)
    k_rope = jnp.concatenate([kr1, kr2, zeros_f], axis=0).T.astype(BF16)

    gqn = jnp.broadcast_to(gqn_ref[...], (QK_NOPE, tm))
    gqr = jnp.broadcast_to(gqr_ref[...], (QK_ROPE, tm))
    gkn = gkn_ref[...]
    zeros = zeros_f.astype(BF16)
    for hd in range(MLA_HEADS):
        lo = hd * (QK_NOPE + QK_ROPE)
        qn = qt[lo:lo + QK_NOPE, :]
        qr = qt[lo + QK_NOPE:lo + QK_NOPE + QK_ROPE, :]
        qn = qn * lax.rsqrt(jnp.sum(qn * qn, axis=0, keepdims=True) * (1.0 / QK_NOPE) + EPS) * gqn
        r1, r2 = _rope_t(qr, gqr, cost, sint)
        half = QK_ROPE // 2
        qt_ref[0, hd, 0:QK_NOPE, :] = qn.astype(BF16)
        qt_ref[0, hd, QK_NOPE:QK_NOPE + half, :] = r1.astype(BF16)
        qt_ref[0, hd, QK_NOPE + half:QK_NOPE + QK_ROPE, :] = r2.astype(BF16)
        qt_ref[0, hd, QK_NOPE + QK_ROPE:HEAD_PAD, :] = zeros
        kh = kn[:, hd * QK_NOPE:(hd + 1) * QK_NOPE]
        k_ref[0, hd, :, 0:QK_NOPE] = (_rms(kh, QK_NOPE) * gkn).astype(BF16)
        k_ref[0, hd, :, QK_NOPE:HEAD_PAD] = k_rope
        vt_ref[0, hd, 0, :, :] = vt[hd * V_HEAD:(hd + 1) * V_HEAD, :].astype(BF16)


def _mla_prep(h, w_front, w_uqt, w_kn, w_vt, gq, gkv, gqn, gqr, gkn, gkr, cost, sint,
              batch, seq, tm, tk):
    t, d = h.shape
    ns = seq // tm
    per_chunk = tk // tm
    lora = Q_LORA + KV_LORA
    const = lambda b, s: (0, 0)
    row = lambda b, s: (b * ns + s, 0)
    col = lambda b, s: (0, b * ns + s)
    qt_shape = jax.ShapeDtypeStruct((batch, MLA_HEADS, HEAD_PAD, seq), BF16)
    k_shape = jax.ShapeDtypeStruct((batch, MLA_HEADS, seq, HEAD_PAD), BF16)
    vt_shape = jax.ShapeDtypeStruct((batch, MLA_HEADS, seq // tk, V_HEAD, tk), BF16)
    full = lambda a: pl.BlockSpec(a.shape, const)
    return pl.pallas_call(
        _mla_prep_kernel,
        out_shape=(qt_shape, k_shape, vt_shape),
        grid=(batch, ns),
        in_specs=[pl.BlockSpec((tm, d), row),
                  pl.BlockSpec((lora, d), lambda b, s: (4 * CONV_WIDTH // lora, 0)),
                  pl.BlockSpec((QK_ROPE, d), lambda b, s: ((4 * CONV_WIDTH + lora) // QK_ROPE, 0)),
                  full(w_uqt), full(w_kn), full(w_vt),
                  full(gq), full(gkv), full(gqn), full(gqr), full(gkn), full(gkr),
                  pl.BlockSpec((QK_ROPE // 2, tm), col), pl.BlockSpec((QK_ROPE // 2, tm), col)],
        out_specs=(pl.BlockSpec((1, MLA_HEADS, HEAD_PAD, tm), lambda b, s: (b, 0, 0, s)),
                   pl.BlockSpec((1, MLA_HEADS, tm, HEAD_PAD), lambda b, s: (b, 0, s, 0)),
                   pl.BlockSpec((1, MLA_HEADS, 1, V_HEAD, tm),
                                lambda b, s: (b, 0, s // per_chunk, 0, s % per_chunk))),
        compiler_params=_params("arbitrary", "arbitrary"),
        name="mla_prep",
    )(h, w_front, w_front, w_uqt, w_kn, w_vt, gq, gkv, gqn, gqr, gkn, gkr, cost, sint)


def _attn_kernel(qt_ref, qtn_ref, k_ref, vt_ref, sz_ref, o_ref, acc_ref, m_ref, s0_ref, s1_ref,
                 *, tq, heads):
    j = pl.program_id(2)
    s_bufs = (s0_ref, s1_ref)

    def scores(hh, ki, q, dst):
        start = pl.multiple_of(ki * tq, tq)
        s = _dot(k_ref[0, hh, pl.ds(start, tq), :], q)
        dst[hh, 0:tq, :] = s
        dst[hh, tq:tq + 1, :] = jnp.max(s, axis=0, keepdims=True)

    def update(hh, ki, src, diagonal):
        s = src[hh, 0:tq, :]
        if diagonal:
            kc = lax.broadcasted_iota(jnp.int32, s.shape, 0) // CHUNK
            qc = lax.broadcasted_iota(jnp.int32, s.shape, 1) // CHUNK
            s = jnp.where(kc <= qc, s, NEG_BIG)
            m_tile = jnp.max(s, axis=0, keepdims=True)
        else:
            m_tile = src[hh, tq:tq + 1, :]
        m = m_ref[hh]
        m_new = jnp.maximum(m, m_tile)
        alpha = jnp.exp2(m - m_new)
        p = jnp.exp2(s - m_new)
        m_ref[hh] = m_new
        vt1 = jnp.concatenate([vt_ref[0, hh, ki], ones_rows], axis=0)
        acc_ref[hh] = alpha * acc_ref[hh] + _dot(vt1, p.astype(BF16))

    ones_rows = (lax.broadcasted_iota(jnp.int32, (BF16_SUBLANES, tq), 0) == 0).astype(BF16)

    @pl.when(j == 0)
    def _():
        for hh in range(heads):
            scores(hh, 0, qt_ref[0, hh, :, 0:tq], s0_ref)

    for u in range(ATTN_GROUP):
        q_tile = ATTN_GROUP * j + u
        first = ((u + 1) // 2) % 2
        bufs = (s_bufs[first], s_bufs[1 - first])

        def q_cur(hh, u=u):
            return qt_ref[0, hh, :, u * tq:(u + 1) * tq]

        def q_next(hh, u=u):
            if u + 1 < ATTN_GROUP:
                return qt_ref[0, hh, :, (u + 1) * tq:(u + 2) * tq]
            return qtn_ref[0, hh]

        acc_ref[...] = jnp.zeros_like(acc_ref)
        m_ref[...] = jnp.full(m_ref.shape, NEG_BIG, F32)

        def step(ki, parity, bufs=bufs, q_cur=q_cur):
            for hh in range(heads):
                scores(hh, ki + 1, q_cur(hh), bufs[1 - parity])
                update(hh, ki, bufs[parity], False)

        def group(jj, carry, step=step):
            for r in range(ATTN_GROUP):
                step(ATTN_GROUP * jj + r, r % 2)
            return carry

        lax.fori_loop(0, j, group, 0)
        for r in range(u):
            step(ATTN_GROUP * j + r, r % 2)
        for hh in range(heads):
            scores(hh, 0, q_next(hh), bufs[1 - u % 2])
            update(hh, q_tile, bufs[u % 2], True)
            l = acc_ref[hh, V_HEAD:V_HEAD + 1, :]
            y = (acc_ref[hh, 0:V_HEAD, :] * (1.0 / l)).T
            sz = sz_ref[0, u * tq:(u + 1) * tq, hh * V_HEAD:(hh + 1) * V_HEAD].astype(F32)
            o_ref[0, u * tq:(u + 1) * tq, hh * V_HEAD:(hh + 1) * V_HEAD] = (y * sz).astype(o_ref.dtype)


def _mla_attention(qt, k, vt, sz, tq, heads):
    batch, n_heads, seq, _ = k.shape
    assert ATTN_GROUP % 4 == 0
    kern = functools.partial(_attn_kernel, tq=tq, heads=heads)
    tg = ATTN_GROUP * tq
    last = seq // tq - 1
    return pl.pallas_call(
        kern,
        out_shape=jax.ShapeDtypeStruct((batch, seq, MLA_WIDTH), BF16),
        grid=(batch, n_heads // heads, seq // tg),
        in_specs=[pl.BlockSpec((1, heads, HEAD_PAD, tg), lambda b, h, i: (b, h, 0, i)),
                  pl.BlockSpec((1, heads, HEAD_PAD, tq),
                               lambda b, h, i: (b, h, 0, jnp.minimum(ATTN_GROUP * (i + 1), last))),
                  pl.BlockSpec((1, heads, seq, HEAD_PAD), lambda b, h, i: (b, h, 0, 0)),
                  pl.BlockSpec((1, heads, seq // tq, V_HEAD, tq), lambda b, h, i: (b, h, 0, 0, 0)),
                  pl.BlockSpec((1, tg, heads * V_HEAD), lambda b, h, i: (b, i, h))],
        out_specs=pl.BlockSpec((1, tg, heads * V_HEAD), lambda b, h, i: (b, i, h)),
        scratch_shapes=[pltpu.VMEM((heads, V_HEAD + BF16_SUBLANES, tq), F32),
                        pltpu.VMEM((heads, 1, tq), F32),
                        pltpu.VMEM((heads, tq + SUBLANES, tq), F32),
                        pltpu.VMEM((heads, tq + SUBLANES, tq), F32)],
        compiler_params=_params("arbitrary", "arbitrary", "arbitrary"),
        name="mla_attention",
    )(qt, qt, k, vt, sz)


def _mem_kv_kernel(mem_ref, g_ref, w_ref, gk_ref, k_ref, v_ref):
    m = mem_ref[0]
    kv = _dot((_rms(m, m.shape[-1]) * g_ref[...]).astype(BF16), w_ref[...].astype(BF16))
    gk = gk_ref[...]
    for hd in range(MEM_HEADS):
        lo = hd * MEM_HEAD_DIM
        k_ref[0, :, lo:lo + MEM_HEAD_DIM] = (_rms(kv[:, lo:lo + MEM_HEAD_DIM], MEM_HEAD_DIM) * gk).astype(BF16)
    v_ref[0] = kv[:, MEM_WIDTH:].astype(BF16)


def _mem_kv(mem, g, w, gk):
    batch, m, d = mem.shape
    const = lambda b: (0, 0)
    shape = jax.ShapeDtypeStruct((batch, m, MEM_WIDTH), BF16)
    return pl.pallas_call(
        _mem_kv_kernel,
        out_shape=(shape, shape),
        grid=(batch,),
        in_specs=[pl.BlockSpec((1, m, d), lambda b: (b, 0, 0)),
                  pl.BlockSpec(g.shape, const),
                  pl.BlockSpec(w.shape, const, pipeline_mode=pl.Buffered(1)),
                  pl.BlockSpec(gk.shape, const)],
        out_specs=(pl.BlockSpec((1, m, MEM_WIDTH), lambda b: (b, 0, 0)),
                   pl.BlockSpec((1, m, MEM_WIDTH), lambda b: (b, 0, 0))),
        compiler_params=_params("arbitrary"),
        name="mem_kv",
    )(mem, g, w, gk)


def _mem_attn_kernel(h_ref, w_ref, k_ref, v_ref, gq_ref, o_ref):
    p = _dot_nt(h_ref[...], w_ref[...])
    gq = gq_ref[...] * (MEM_HEAD_DIM ** -0.5)
    for hd in range(MEM_HEADS):
        lo = hd * MEM_HEAD_DIM
        q = (_rms(p[:, lo:lo + MEM_HEAD_DIM], MEM_HEAD_DIM) * gq).astype(BF16)
        z = p[:, MEM_WIDTH + lo:MEM_WIDTH + lo + MEM_HEAD_DIM]
        s = _dot_nt(q, k_ref[0, :, lo:lo + MEM_HEAD_DIM])
        e = jnp.exp(s - jnp.max(s, axis=-1, keepdims=True))
        y = _dot(e.astype(BF16), v_ref[0, :, lo:lo + MEM_HEAD_DIM])
        y = y * (1.0 / jnp.sum(e, axis=-1, keepdims=True))
        o_ref[:, lo:lo + MEM_HEAD_DIM] = (y * (z * _sigmoid(z))).astype(o_ref.dtype)


def _mem_attention(h, w, col0, k_mem, v_mem, gq, batch, seq, tm):
    t, d = h.shape
    ns = seq // tm
    m = k_mem.shape[1]
    const = lambda b, s: (0, 0)
    return pl.pallas_call(
        _mem_attn_kernel,
        out_shape=jax.ShapeDtypeStruct((t, MEM_WIDTH), BF16),
        grid=(batch, ns),
        in_specs=[pl.BlockSpec((tm, d), lambda b, s: (b * ns + s, 0)),
                  pl.BlockSpec((pl.Element(2 * MEM_WIDTH), pl.Element(d)), lambda b, s: (col0, 0)),
                  pl.BlockSpec((1, m, MEM_WIDTH), lambda b, s: (b, 0, 0)),
                  pl.BlockSpec((1, m, MEM_WIDTH), lambda b, s: (b, 0, 0)),
                  pl.BlockSpec(gq.shape, const)],
        out_specs=pl.BlockSpec((tm, MEM_WIDTH), lambda b, s: (b * ns + s, 0)),
        compiler_params=_params("arbitrary", "arbitrary"),
        name="mem_attention",
    )(h, w, k_mem, v_mem, gq)


def _gate_kernel(h_ref, w_ref, *refs, silu):
    n_cast = (len(refs) - 1) // 2
    o_ref = refs[n_cast]
    for src, dst in zip(refs[:n_cast], refs[n_cast + 1:]):
        dst[...] = src[...].astype(BF16)
    p = _dot_nt(h_ref[...], w_ref[...])
    s = _sigmoid(p)
    o_ref[...] = ((p * s) if silu else s).astype(o_ref.dtype)


def _gate_proj(h, w, col0, n, silu, tm, tn, cast=()):
    t, d = h.shape
    assert col0 % BF16_SUBLANES == 0 and n % tn == 0
    nj = n // tn
    n_steps = (t // tm) * nj
    slabs = []
    for a in cast:
        assert a.shape[0] % (n_steps * BF16_SUBLANES) == 0
        slabs.append(pl.BlockSpec((a.shape[0] // n_steps, a.shape[1]), lambda i, j: (i * nj + j, 0)))
    out = pl.pallas_call(
        functools.partial(_gate_kernel, silu=silu),
        out_shape=(jax.ShapeDtypeStruct((t, n), BF16), *[jax.ShapeDtypeStruct(a.shape, BF16) for a in cast]),
        grid=(t // tm, nj),
        in_specs=[pl.BlockSpec((tm, d), lambda i, j: (i, 0)),
                  pl.BlockSpec((pl.Element(tn), pl.Element(d)),
                               lambda i, j: (pl.multiple_of(col0 + j * tn, BF16_SUBLANES), 0)),
                  *slabs],
        out_specs=(pl.BlockSpec((tm, tn), lambda i, j: (i, j)), *slabs),
        compiler_params=_params("arbitrary", "arbitrary"),
        name="silu_gate_proj" if silu else "sigmoid_gate_proj",
    )(h, w, *cast)
    return out if cast else out[0]


def _merge_out_kernel(ac_ref, am_ref, ae_ref, gc_ref, gm_ref, ge_ref, x_ref, wc_ref, wm_ref, we_ref,
                      wo_ref, o_ref):
    merged = (gc_ref[...].astype(F32) * _dot(ac_ref[...], wc_ref[...])
              + gm_ref[...].astype(F32) * _dot(am_ref[...], wm_ref[...])
              + ge_ref[...].astype(F32) * _dot(ae_ref[...], we_ref[...]))
    o_ref[...] = x_ref[...] + _dot(merged.astype(BF16), wo_ref[...])


def _merge_out(a_conv, a_mla, a_mem, gates, x2d, w_conv, w_mla, w_mem, w_o, tm):
    t, d = x2d.shape
    row = lambda a: pl.BlockSpec((tm, a.shape[1]), lambda i: (i, 0))
    gate = lambda k: pl.BlockSpec((tm, d), lambda i: (i, k))
    wgt = lambda w: pl.BlockSpec(w.shape, lambda i: (0, 0), pipeline_mode=pl.Buffered(1))
    return pl.pallas_call(
        _merge_out_kernel,
        out_shape=jax.ShapeDtypeStruct((t, d), F32),
        grid=(t // tm,),
        in_specs=[row(a_conv), row(a_mla), row(a_mem), gate(0), gate(1), gate(2), row(x2d),
                  wgt(w_conv), wgt(w_mla), wgt(w_mem), wgt(w_o)],
        out_specs=pl.BlockSpec((tm, d), lambda i: (i, 0)),
        compiler_params=pltpu.CompilerParams(dimension_semantics=("arbitrary",),
                                             vmem_limit_bytes=MERGE_VMEM_LIMIT_BYTES),
        name="merge_out",
    )(a_conv, a_mla, a_mem, gates, gates, gates, x2d, w_conv, w_mla, w_mem, w_o)


def _layer(x, cost, sint, mem, norm_g, w_in, conv_w, w_conv_out, mla_q_norm_g, w_uq, mla_kv_norm_g,
           w_ukv, mla_qn_nope_g, mla_qn_rope_g, mla_kn_nope_g, mla_kn_rope_g, w_mla_out, mem_norm_g,
           w_mem_kv, mem_qn_g, mem_kn_g, w_mem_out, w_o):
    batch, seq, d = x.shape
    t = batch * seq
    x2d = x.reshape(t, d)

    o_rest = 4 * CONV_WIDTH + Q_LORA + KV_LORA + QK_ROPE
    w_in_t = w_in.T
    w_front = w_in_t[:o_rest].astype(BF16)
    w_uqt = w_uq.astype(BF16).T
    w_ukv_h = w_ukv.astype(BF16).reshape(KV_LORA, MLA_HEADS, QK_NOPE + V_HEAD)
    w_kn = w_ukv_h[:, :, :QK_NOPE].reshape(KV_LORA, MLA_HEADS * QK_NOPE)
    w_vt = w_ukv_h[:, :, QK_NOPE:].reshape(KV_LORA, MLA_WIDTH).T
    qscale = (QK_NOPE + QK_ROPE) ** -0.5 * LOG2_E

    conv_act, h, w_rest = _norm_conv_branch(x2d, norm_g[None, :], w_front, conv_w, w_in_t, o_rest, seq,
                                            tm=PROJ_ROW_TILE)
    qt, k, vt = _mla_prep(h, w_front, w_uqt, w_kn, w_vt, mla_q_norm_g[None, :],
                          mla_kv_norm_g[None, :], (mla_qn_nope_g * qscale)[:, None],
                          (mla_qn_rope_g * qscale)[:, None], mla_kn_nope_g[None, :],
                          mla_kn_rope_g[:, None], cost, sint, batch, seq, tm=PREP_ROW_TILE, tk=ATTN_TILE)
    sz_mla, wb_conv, wb_mla, wb_mem, wb_o = _gate_proj(
        h, w_rest, 0, MLA_WIDTH, True, tm=PROJ_ROW_TILE, tn=GATE_COL_TILE,
        cast=(w_conv_out, w_mla_out, w_mem_out, w_o))
    gates = _gate_proj(h, w_rest, MLA_WIDTH + 2 * MEM_WIDTH, 3 * d, False, tm=PROJ_ROW_TILE,
                       tn=GATE_COL_TILE)
    mla_act = _mla_attention(qt, k, vt, sz_mla.reshape(batch, seq, MLA_WIDTH), tq=ATTN_TILE,
                             heads=ATTN_HEADS_PER_STEP)
    k_mem, v_mem = _mem_kv(mem, mem_norm_g[None, :], w_mem_kv, mem_kn_g[None, :])
    mem_act = _mem_attention(h, w_rest, MLA_WIDTH, k_mem, v_mem, mem_qn_g[None, :], batch, seq,
                             tm=MEM_ROW_TILE)
    out = _merge_out(conv_act, mla_act.reshape(t, MLA_WIDTH), mem_act, gates, x2d, wb_conv, wb_mla, wb_mem,
                     wb_o, tm=MERGE_ROW_TILE)
    return out.reshape(batch, seq, d)


def kernel(x, positions, mem, norm_g, w_in, conv_w, w_conv_out, mla_q_norm_g, w_uq, mla_kv_norm_g, w_ukv, mla_qn_nope_g, mla_qn_rope_g, mla_kn_nope_g, mla_kn_rope_g, w_mla_out, mem_norm_g, w_mem_kv, mem_qn_g, mem_kn_g, w_mem_out, w_o):
    batch, seq, _ = x.shape
    half = QK_ROPE // 2
    inv_freq = jnp.power(ROPE_THETA, -jnp.arange(half, dtype=F32) / half)
    ang = positions.astype(F32)[..., None] * inv_freq
    cost = jnp.cos(ang).reshape(batch * seq, half).T
    sint = jnp.sin(ang).reshape(batch * seq, half).T
    for l in range(norm_g.shape[0]):
        x = _layer(x, cost, sint, mem, norm_g[l], w_in[l], conv_w[l], w_conv_out[l], mla_q_norm_g[l],
                   w_uq[l], mla_kv_norm_g[l], w_ukv[l], mla_qn_nope_g[l], mla_qn_rope_g[l],
                   mla_kn_nope_g[l], mla_kn_rope_g[l], w_mla_out[l], mem_norm_g[l], w_mem_kv[l],
                   mem_qn_g[l], mem_kn_g[l], w_mem_out[l], w_o[l])
    return x
```

```python
import functools

import jax
import jax.numpy as jnp
from jax import lax
from jax.experimental import pallas as pl
from jax.experimental.pallas import tpu as pltpu

F32 = jnp.float32
BF16 = jnp.bfloat16

EPS = 1e-6
CHUNK = 64
CONV_WIDTH = 1024
CONV_K = 3
CONV_GROUP = 256
MLA_HEADS = 16
Q_LORA = 512
KV_LORA = 512
QK_NOPE = 128
QK_ROPE = 64
V_HEAD = 128
MLA_WIDTH = MLA_HEADS * V_HEAD
HEAD_PAD = 256
ROPE_THETA = 10000.0
MEM_HEADS = 4
MEM_HEAD_DIM = 256
MEM_WIDTH = MEM_HEADS * MEM_HEAD_DIM
NEG_BIG = -1e30
LOG2_E = 1.4426950408889634
ATTN_TILE = 512
ATTN_GROUP = 4
ATTN_HEADS_PER_STEP = 2

PROJ_ROW_TILE = 1024
GATE_COL_TILE = 2048
MEM_ROW_TILE = 1024
PREP_ROW_TILE = 256
MERGE_ROW_TILE = 256

SUBLANES = 8
BF16_SUBLANES = 16
VMEM_LIMIT_BYTES = 52 * 1024 * 1024
MERGE_VMEM_LIMIT_BYTES = 50 * 1024 * 1024


def _params(*sem):
    return pltpu.CompilerParams(dimension_semantics=sem, vmem_limit_bytes=VMEM_LIMIT_BYTES)


def _rms(x, width):
    ms = jnp.sum(x * x, axis=-1, keepdims=True) * (1.0 / width)
    return x * lax.rsqrt(ms + EPS)


def _sigmoid(x):
    return 0.5 * jnp.tanh(0.5 * x) + 0.5


def _dot(a, b):
    return jnp.dot(a, b, preferred_element_type=F32)


def _dot_nt(a, b):
    return lax.dot_general(a, b, (((1,), (1,)), ((), ())), preferred_element_type=F32)


def _conv_kernel(x_ref, ng_ref, wc_ref, wb_ref, wu_ref, wz_ref, cw_ref, wf_ref, o_ref, h_ref, wr_ref,
                 carry_ref, buf_ref, *, tm, tiles_per_seq):
    i = pl.program_id(0)
    g = pl.program_id(1)
    wr_ref[...] = wf_ref[...].astype(BF16)

    @pl.when(g == 0)
    def _():
        x = x_ref[...]
        h_ref[...] = (_rms(x, x.shape[-1]) * ng_ref[...]).astype(BF16)

    h = h_ref[...]
    b = _dot_nt(h, wb_ref[...])
    z = _dot_nt(h, wz_ref[...])
    cu = _dot_nt(h, wc_ref[...]) * _dot_nt(h, wu_ref[...])
    seq_start = (i % tiles_per_seq) == 0
    buf_ref[0:SUBLANES, :] = jnp.where(seq_start, 0.0, carry_ref[g])
    buf_ref[SUBLANES:SUBLANES + tm, :] = cu
    carry_ref[g] = cu[tm - SUBLANES:tm, :]
    cw = cw_ref[...]
    y = cu * cw[CONV_K - 1:CONV_K, :]
    for tap in range(CONV_K - 1):
        lag = CONV_K - 1 - tap
        y = y + buf_ref[SUBLANES - lag:SUBLANES - lag + tm, :] * cw[tap:tap + 1, :]
    o_ref[...] = (b * y * (z * _sigmoid(z))).astype(o_ref.dtype)


def _norm_conv_branch(x2d, norm_g, w_front, conv_w, w_in_t, row0, seq, tm):
    t, d = x2d.shape
    n_groups = CONV_WIDTH // CONV_GROUP
    n_steps = (t // tm) * n_groups
    rest = w_in_t.shape[0] - row0
    slab = rest // n_steps
    assert slab * n_steps == rest and slab % BF16_SUBLANES == 0 and row0 % SUBLANES == 0
    kern = functools.partial(_conv_kernel, tm=tm, tiles_per_seq=seq // tm)
    part = lambda k: pl.BlockSpec((CONV_GROUP, d), lambda i, g: (k * n_groups + g, 0))
    return pl.pallas_call(
        kern,
        out_shape=(jax.ShapeDtypeStruct((t, CONV_WIDTH), BF16), jax.ShapeDtypeStruct((t, d), BF16),
                   jax.ShapeDtypeStruct((rest, d), BF16)),
        grid=(t // tm, n_groups),
        in_specs=[pl.BlockSpec((tm, d), lambda i, g: (i, 0)),
                  pl.BlockSpec((1, d), lambda i, g: (0, 0)),
                  part(0), part(1), part(2), part(3),
                  pl.BlockSpec((CONV_K, CONV_GROUP), lambda i, g: (0, g)),
                  pl.BlockSpec((pl.Element(slab), pl.Element(d)),
                               lambda i, g: (pl.multiple_of(row0 + (i * n_groups + g) * slab, SUBLANES), 0))],
        out_specs=(pl.BlockSpec((tm, CONV_GROUP), lambda i, g: (i, g)),
                   pl.BlockSpec((tm, d), lambda i, g: (i, 0)),
                   pl.BlockSpec((slab, d), lambda i, g: (i * n_groups + g, 0))),
        scratch_shapes=[pltpu.VMEM((n_groups, SUBLANES, CONV_GROUP), F32),
                        pltpu.VMEM((tm + SUBLANES, CONV_GROUP), F32)],
        compiler_params=_params("arbitrary", "arbitrary"),
        name="norm_conv_branch",
    )(x2d, norm_g, w_front, w_front, w_front, w_front, conv_w, w_in_t)


def _rope_t(x, g, cost, sint):
    half = QK_ROPE // 2
    x = x * lax.rsqrt(jnp.sum(x * x, axis=0, keepdims=True) * (1.0 / QK_ROPE) + EPS) * g
    x1 = x[0:half, :]
    x2 = x[half:QK_ROPE, :]
    return x1 * cost - x2 * sint, x2 * cost + x1 * sint


def _mla_prep_kernel(h_ref, win_ref, wkrt_ref, wuqt_ref, wkn_ref, wvt_ref, gq_ref, gkv_ref, gqn_ref,
                     gqr_ref, gkn_ref, gkr_ref, cost_ref, sint_ref, qt_ref, k_ref, vt_ref):
    tm = h_ref.shape[0]
    h = h_ref[...]
    p = _dot_nt(h, win_ref[...])
    cqn = (_rms(p[:, 0:Q_LORA], Q_LORA) * gq_ref[...]).astype(BF16)
    ckvn = (_rms(p[:, Q_LORA:Q_LORA + KV_LORA], KV_LORA) * gkv_ref[...]).astype(BF16)
    qt = _dot_nt(wuqt_ref[...], cqn)
    kn = _dot(ckvn, wkn_ref[...])
    vt = _dot_nt(wvt_ref[...], ckvn)
    krt = _dot_nt(wkrt_ref[...], h)

    cost = cost_ref[...]
    sint = sint_ref[...]
    kr1, kr2 = _rope_t(krt, jnp.broadcast_to(gkr_ref[...], (QK_ROPE, tm)), cost, sint)
    zeros_f = jnp.zeros((HEAD_PAD - QK_NOPE - QK_ROPE, tm), F32)
    k_rope = jnp.concatenate([kr1, kr2, zeros_f], axis=0).T.astype(BF16)

    gqn = jnp.broadcast_to(gqn_ref[...], (QK_NOPE, tm))
    gqr = jnp.broadcast_to(gqr_ref[...], (QK_ROPE, tm))
    gkn = gkn_ref[...]
    zeros = zeros_f.astype(BF16)
    for hd in range(MLA_HEADS):
        lo = hd * (QK_NOPE + QK_ROPE)
        qn = qt[lo:lo + QK_NOPE, :]
        qr = qt[lo + QK_NOPE:lo + QK_NOPE + QK_ROPE, :]
        qn = qn * lax.rsqrt(jnp.sum(qn * qn, axis=0, keepdims=True) * (1.0 / QK_NOPE) + EPS) * gqn
        r1, r2 = _rope_t(qr, gqr, cost, sint)
        half = QK_ROPE // 2
        qt_ref[0, hd, 0:QK_NOPE, :] = qn.astype(BF16)
        qt_ref[0, hd, QK_NOPE:QK_NOPE + half, :] = r1.astype(BF16)
        qt_ref[0, hd, QK_NOPE + half:QK_NOPE + QK_ROPE, :] = r2.astype(BF16)
        qt_ref[0, hd, QK_NOPE + QK_ROPE:HEAD_PAD, :] = zeros
        kh = kn[:, hd * QK_NOPE:(hd + 1) * QK_NOPE]
        k_ref[0, hd, :, 0:QK_NOPE] = (_rms(kh, QK_NOPE) * gkn).astype(BF16)
        k_ref[0, hd, :, QK_NOPE:HEAD_PAD] = k_rope
        vt_ref[0, hd, 0, :, :] = vt[hd * V_HEAD:(hd + 1) * V_HEAD, :].astype(BF16)


def _mla_prep(h, w_front, w_uqt, w_kn, w_vt, gq, gkv, gqn, gqr, gkn, gkr, cost, sint,
              batch, seq, tm, tk):
    t, d = h.shape
    ns = seq // tm
    per_chunk = tk // tm
    lora = Q_LORA + KV_LORA
    const = lambda b, s: (0, 0)
    row = lambda b, s: (b * ns + s, 0)
    col = lambda b, s: (0, b * ns + s)
    qt_shape = jax.ShapeDtypeStruct((batch, MLA_HEADS, HEAD_PAD, seq), BF16)
    k_shape = jax.ShapeDtypeStruct((batch, MLA_HEADS, seq, HEAD_PAD), BF16)
    vt_shape = jax.ShapeDtypeStruct((batch, MLA_HEADS, seq // tk, V_HEAD, tk), BF16)
    full = lambda a: pl.BlockSpec(a.shape, const)
    return pl.pallas_call(
        _mla_prep_kernel,
        out_shape=(qt_shape, k_shape, vt_shape),
        grid=(batch, ns),
        in_specs=[pl.BlockSpec((tm, d), row),
                  pl.BlockSpec((lora, d), lambda b, s: (4 * CONV_WIDTH // lora, 0)),
                  pl.BlockSpec((QK_ROPE, d), lambda b, s: ((4 * CONV_WIDTH + lora) // QK_ROPE, 0)),
                  full(w_uqt), full(w_kn), full(w_vt),
                  full(gq), full(gkv), full(gqn), full(gqr), full(gkn), full(gkr),
                  pl.BlockSpec((QK_ROPE // 2, tm), col), pl.BlockSpec((QK_ROPE // 2, tm), col)],
        out_specs=(pl.BlockSpec((1, MLA_HEADS, HEAD_PAD, tm), lambda b, s: (b, 0, 0, s)),
                   pl.BlockSpec((1, MLA_HEADS, tm, HEAD_PAD), lambda b, s: (b, 0, s, 0)),
                   pl.BlockSpec((1, MLA_HEADS, 1, V_HEAD, tm),
                                lambda b, s: (b, 0, s // per_chunk, 0, s % per_chunk))),
        compiler_params=_params("arbitrary", "arbitrary"),
        name="mla_prep",
    )(h, w_front, w_front, w_uqt, w_kn, w_vt, gq, gkv, gqn, gqr, gkn, gkr, cost, sint)


def _attn_kernel(qt_ref, qtn_ref, k_ref, vt_ref, sz_ref, o_ref, acc_ref, m_ref, s0_ref, s1_ref,
                 *, tq, heads):
    j = pl.program_id(2)
    s_bufs = (s0_ref, s1_ref)

    def scores(hh, ki, q, dst):
        start = pl.multiple_of(ki * tq, tq)
        s = _dot(k_ref[0, hh, pl.ds(start, tq), :], q)
        dst[hh, 0:tq, :] = s
        dst[hh, tq:tq + 1, :] = jnp.max(s, axis=0, keepdims=True)

    def update(hh, ki, src, diagonal):
        s = src[hh, 0:tq, :]
        if diagonal:
            kc = lax.broadcasted_iota(jnp.int32, s.shape, 0) // CHUNK
            qc = lax.broadcasted_iota(jnp.int32, s.shape, 1) // CHUNK
            s = jnp.where(kc <= qc, s, NEG_BIG)
            m_tile = jnp.max(s, axis=0, keepdims=True)
        else:
            m_tile = src[hh, tq:tq + 1, :]
        m = m_ref[hh]
        m_new = jnp.maximum(m, m_tile)
        alpha = jnp.exp2(m - m_new)
        p = jnp.exp2(s - m_new)
        m_ref[hh] = m_new
        vt1 = jnp.concatenate([vt_ref[0, hh, ki], ones_rows], axis=0)
        acc_ref[hh] = alpha * acc_ref[hh] + _dot(vt1, p.astype(BF16))

    ones_rows = (lax.broadcasted_iota(jnp.int32, (BF16_SUBLANES, tq), 0) == 0).astype(BF16)

    @pl.when(j == 0)
    def _():
        for hh in range(heads):
            scores(hh, 0, qt_ref[0, hh, :, 0:tq], s0_ref)

    for u in range(ATTN_GROUP):
        q_tile = ATTN_GROUP * j + u
        first = ((u + 1) // 2) % 2
        bufs = (s_bufs[first], s_bufs[1 - first])

        def q_cur(hh, u=u):
            return qt_ref[0, hh, :, u * tq:(u + 1) * tq]

        def q_next(hh, u=u):
            if u + 1 < ATTN_GROUP:
                return qt_ref[0, hh, :, (u + 1) * tq:(u + 2) * tq]
            return qtn_ref[0, hh]

        acc_ref[...] = jnp.zeros_like(acc_ref)
        m_ref[...] = jnp.full(m_ref.shape, NEG_BIG, F32)

        def step(ki, parity, bufs=bufs, q_cur=q_cur):
            for hh in range(heads):
                scores(hh, ki + 1, q_cur(hh), bufs[1 - parity])
                update(hh, ki, bufs[parity], False)

        def group(jj, carry, step=step):
            for r in range(ATTN_GROUP):
                step(ATTN_GROUP * jj + r, r % 2)
            return carry

        lax.fori_loop(0, j, group, 0)
        for r in range(u):
            step(ATTN_GROUP * j + r, r % 2)
        for hh in range(heads):
            scores(hh, 0, q_next(hh), bufs[1 - u % 2])
            update(hh, q_tile, bufs[u % 2], True)
            l = acc_ref[hh, V_HEAD:V_HEAD + 1, :]
            y = (acc_ref[hh, 0:V_HEAD, :] * (1.0 / l)).T
            sz = sz_ref[0, u * tq:(u + 1) * tq, hh * V_HEAD:(hh + 1) * V_HEAD].astype(F32)
            o_ref[0, u * tq:(u + 1) * tq, hh * V_HEAD:(hh + 1) * V_HEAD] = (y * sz).astype(o_ref.dtype)


def _mla_attention(qt, k, vt, sz, tq, heads):
    batch, n_heads, seq, _ = k.shape
    assert ATTN_GROUP % 4 == 0
    kern = functools.partial(_attn_kernel, tq=tq, heads=heads)
    tg = ATTN_GROUP * tq
    last = seq // tq - 1
    return pl.pallas_call(
        kern,
        out_shape=jax.ShapeDtypeStruct((batch, seq, MLA_WIDTH), BF16),
        grid=(batch, n_heads // heads, seq // tg),
        in_specs=[pl.BlockSpec((1, heads, HEAD_PAD, tg), lambda b, h, i: (b, h, 0, i)),
                  pl.BlockSpec((1, heads, HEAD_PAD, tq),
                               lambda b, h, i: (b, h, 0, jnp.minimum(ATTN_GROUP * (i + 1), last))),
                  pl.BlockSpec((1, heads, seq, HEAD_PAD), lambda b, h, i: (b, h, 0, 0)),
                  pl.BlockSpec((1, heads, seq // tq, V_HEAD, tq), lambda b, h, i: (b, h, 0, 0, 0)),
                  pl.BlockSpec((1, tg, heads * V_HEAD), lambda b, h, i: (b, i, h))],
        out_specs=pl.BlockSpec((1, tg, heads * V_HEAD), lambda b, h, i: (b, i, h)),
        scratch_shapes=[pltpu.VMEM((heads, V_HEAD + BF16_SUBLANES, tq), F32),
                        pltpu.VMEM((heads, 1, tq), F32),
                        pltpu.VMEM((heads, tq + SUBLANES, tq), F32),
                        pltpu.VMEM((heads, tq + SUBLANES, tq), F32)],
        compiler_params=_params("arbitrary", "arbitrary", "arbitrary"),
        name="mla_attention",
    )(qt, qt, k, vt, sz)


def _mem_kv_kernel(mem_ref, g_ref, w_ref, gk_ref, k_ref, v_ref):
    m = mem_ref[0]
    kv = _dot((_rms(m, m.shape[-1]) * g_ref[...]).astype(BF16), w_ref[...].astype(BF16))
    gk = gk_ref[...]
    for hd in range(MEM_HEADS):
        lo = hd * MEM_HEAD_DIM
        k_ref[0, :, lo:lo + MEM_HEAD_DIM] = (_rms(kv[:, lo:lo + MEM_HEAD_DIM], MEM_HEAD_DIM) * gk).astype(BF16)
    v_ref[0] = kv[:, MEM_WIDTH:].astype(BF16)


def _mem_kv(mem, g, w, gk):
    batch, m, d = mem.shape
    const = lambda b: (0, 0)
    shape = jax.ShapeDtypeStruct((batch, m, MEM_WIDTH), BF16)
    return pl.pallas_call(
        _mem_kv_kernel,
        out_shape=(shape, shape),
        grid=(batch,),
        in_specs=[pl.BlockSpec((1, m, d), lambda b: (b, 0, 0)),
                  pl.BlockSpec(g.shape, const),
                  pl.BlockSpec(w.shape, const, pipeline_mode=pl.Buffered(1)),
                  pl.BlockSpec(gk.shape, const)],
        out_specs=(pl.BlockSpec((1, m, MEM_WIDTH), lambda b: (b, 0, 0)),
                   pl.BlockSpec((1, m, MEM_WIDTH), lambda b: (b, 0, 0))),
        compiler_params=_params("arbitrary"),
        name="mem_kv",
    )(mem, g, w, gk)


def _mem_attn_kernel(h_ref, w_ref, k_ref, v_ref, gq_ref, o_ref):
    p = _dot_nt(h_ref[...], w_ref[...])
    gq = gq_ref[...] * (MEM_HEAD_DIM ** -0.5)
    for hd in range(MEM_HEADS):
        lo = hd * MEM_HEAD_DIM
        q = (_rms(p[:, lo:lo + MEM_HEAD_DIM], MEM_HEAD_DIM) * gq).astype(BF16)
        z = p[:, MEM_WIDTH + lo:MEM_WIDTH + lo + MEM_HEAD_DIM]
        s = _dot_nt(q, k_ref[0, :, lo:lo + MEM_HEAD_DIM])
        e = jnp.exp(s - jnp.max(s, axis=-1, keepdims=True))
        y = _dot(e.astype(BF16), v_ref[0, :, lo:lo + MEM_HEAD_DIM])
        y = y * (1.0 / jnp.sum(e, axis=-1, keepdims=True))
        o_ref[:, lo:lo + MEM_HEAD_DIM] = (y * (z * _sigmoid(z))).astype(o_ref.dtype)


def _mem_attention(h, w, col0, k_mem, v_mem, gq, batch, seq, tm):
    t, d = h.shape
    ns = seq // tm
    m = k_mem.shape[1]
    const = lambda b, s: (0, 0)
    return pl.pallas_call(
        _mem_attn_kernel,
        out_shape=jax.ShapeDtypeStruct((t, MEM_WIDTH), BF16),
        grid=(batch, ns),
        in_specs=[pl.BlockSpec((tm, d), lambda b, s: (b * ns + s, 0)),
                  pl.BlockSpec((pl.Element(2 * MEM_WIDTH), pl.Element(d)), lambda b, s: (col0, 0)),
                  pl.BlockSpec((1, m, MEM_WIDTH), lambda b, s: (b, 0, 0)),
                  pl.BlockSpec((1, m, MEM_WIDTH), lambda b, s: (b, 0, 0)),
                  pl.BlockSpec(gq.shape, const)],
        out_specs=pl.BlockSpec((tm, MEM_WIDTH), lambda b, s: (b * ns + s, 0)),
        compiler_params=_params("arbitrary", "arbitrary"),
        name="mem_attention",
    )(h, w, k_mem, v_mem, gq)


def _gate_kernel(h_ref, w_ref, *refs, silu):
    n_cast = (len(refs) - 1) // 2
    o_ref = refs[n_cast]
    for src, dst in zip(refs[:n_cast], refs[n_cast + 1:]):
        dst[...] = src[...].astype(BF16)
    p = _dot_nt(h_ref[...], w_ref[...])
    s = _sigmoid(p)
    o_ref[...] = ((p * s) if silu else s).astype(o_ref.dtype)


def _gate_proj(h, w, col0, n, silu, tm, tn, cast=()):
    t, d = h.shape
    assert col0 % BF16_SUBLANES == 0 and n % tn == 0
    ni, nj = t // tm, n // tn
    n_steps = ni * nj
    slabs = []
    for a in cast:
        assert a.shape[0] % (n_steps * BF16_SUBLANES) == 0
        slabs.append(pl.BlockSpec((a.shape[0] // n_steps, a.shape[1]), lambda j, i: (j * ni + i, 0)))
    out = pl.pallas_call(
        functools.partial(_gate_kernel, silu=silu),
        out_shape=(jax.ShapeDtypeStruct((t, n), BF16), *[jax.ShapeDtypeStruct(a.shape, BF16) for a in cast]),
        grid=(nj, ni),
        in_specs=[pl.BlockSpec((tm, d), lambda j, i: (i, 0)),
                  pl.BlockSpec((pl.Element(tn), pl.Element(d)),
                               lambda j, i: (pl.multiple_of(col0 + j * tn, BF16_SUBLANES), 0)),
                  *slabs],
        out_specs=(pl.BlockSpec((tm, tn), lambda j, i: (i, j)), *slabs),
        compiler_params=_params("arbitrary", "arbitrary"),
        name="silu_gate_proj" if silu else "sigmoid_gate_proj",
    )(h, w, *cast)
    return out if cast else out[0]


def _merge_out_kernel(ac_ref, am_ref, ae_ref, gc_ref, gm_ref, ge_ref, x_ref, wc_ref, wm_ref, we_ref,
                      wo_ref, o_ref):
    merged = (gc_ref[...].astype(F32) * _dot(ac_ref[...], wc_ref[...])
              + gm_ref[...].astype(F32) * _dot(am_ref[...], wm_ref[...])
              + ge_ref[...].astype(F32) * _dot(ae_ref[...], we_ref[...]))
    o_ref[...] = x_ref[...] + _dot(merged.astype(BF16), wo_ref[...])


def _merge_out(a_conv, a_mla, a_mem, gates, x2d, w_conv, w_mla, w_mem, w_o, tm):
    t, d = x2d.shape
    row = lambda a: pl.BlockSpec((tm, a.shape[1]), lambda i: (i, 0))
    gate = lambda k: pl.BlockSpec((tm, d), lambda i: (i, k))
    wgt = lambda w: pl.BlockSpec(w.shape, lambda i: (0, 0), pipeline_mode=pl.Buffered(1))
    return pl.pallas_call(
        _merge_out_kernel,
        out_shape=jax.ShapeDtypeStruct((t, d), F32),
        grid=(t // tm,),
        in_specs=[row(a_conv), row(a_mla), row(a_mem), gate(0), gate(1), gate(2), row(x2d),
                  wgt(w_conv), wgt(w_mla), wgt(w_mem), wgt(w_o)],
        out_specs=pl.BlockSpec((tm, d), lambda i: (i, 0)),
        compiler_params=pltpu.CompilerParams(dimension_semantics=("arbitrary",),
                                             vmem_limit_bytes=MERGE_VMEM_LIMIT_BYTES),
        name="merge_out",
    )(a_conv, a_mla, a_mem, gates, gates, gates, x2d, w_conv, w_mla, w_mem, w_o)


def _layer(x, cost, sint, mem, norm_g, w_in, conv_w, w_conv_out, mla_q_norm_g, w_uq, mla_kv_norm_g,
           w_ukv, mla_qn_nope_g, mla_qn_rope_g, mla_kn_nope_g, mla_kn_rope_g, w_mla_out, mem_norm_g,
           w_mem_kv, mem_qn_g, mem_kn_g, w_mem_out, w_o):
    batch, seq, d = x.shape
    t = batch * seq
    x2d = x.reshape(t, d)

    o_rest = 4 * CONV_WIDTH + Q_LORA + KV_LORA + QK_ROPE
    w_in_t = w_in.T
    w_front = w_in_t[:o_rest].astype(BF16)
    w_uqt = w_uq.astype(BF16).T
    w_ukv_h = w_ukv.astype(BF16).reshape(KV_LORA, MLA_HEADS, QK_NOPE + V_HEAD)
    w_kn = w_ukv_h[:, :, :QK_NOPE].reshape(KV_LORA, MLA_HEADS * QK_NOPE)
    w_vt = w_ukv_h[:, :, QK_NOPE:].reshape(KV_LORA, MLA_WIDTH).T
    qscale = (QK_NOPE + QK_ROPE) ** -0.5 * LOG2_E

    conv_act, h, w_rest = _norm_conv_branch(x2d, norm_g[None, :], w_front, conv_w, w_in_t, o_rest, seq,
                                            tm=PROJ_ROW_TILE)
    qt, k, vt = _mla_prep(h, w_front, w_uqt, w_kn, w_vt, mla_q_norm_g[None, :],
                          mla_kv_norm_g[None, :], (mla_qn_nope_g * qscale)[:, None],
                          (mla_qn_rope_g * qscale)[:, None], mla_kn_nope_g[None, :],
                          mla_kn_rope_g[:, None], cost, sint, batch, seq, tm=PREP_ROW_TILE, tk=ATTN_TILE)
    sz_mla, wb_conv, wb_mla, wb_mem, wb_o = _gate_proj(
        h, w_rest, 0, MLA_WIDTH, True, tm=PROJ_ROW_TILE, tn=GATE_COL_TILE,
        cast=(w_conv_out, w_mla_out, w_mem_out, w_o))
    gates = _gate_proj(h, w_rest, MLA_WIDTH + 2 * MEM_WIDTH, 3 * d, False, tm=PROJ_ROW_TILE,
                       tn=GATE_COL_TILE)
    mla_act = _mla_attention(qt, k, vt, sz_mla.reshape(batch, seq, MLA_WIDTH), tq=ATTN_TILE,
                             heads=ATTN_HEADS_PER_STEP)
    k_mem, v_mem = _mem_kv(mem, mem_norm_g[None, :], w_mem_kv, mem_kn_g[None, :])
    mem_act = _mem_attention(h, w_rest, MLA_WIDTH, k_mem, v_mem, mem_qn_g[None, :], batch, seq,
                             tm=MEM_ROW_TILE)
    out = _merge_out(conv_act, mla_act.reshape(t, MLA_WIDTH), mem_act, gates, x2d, wb_conv, wb_mla, wb_mem,
                     wb_o, tm=MERGE_ROW_TILE)
    return out.reshape(batch, seq, d)


def kernel(x, positions, mem, norm_g, w_in, conv_w, w_conv_out, mla_q_norm_g, w_uq, mla_kv_norm_g, w_ukv, mla_qn_nope_g, mla_qn_rope_g, mla_kn_nope_g, mla_kn_rope_g, w_mla_out, mem_norm_g, w_mem_kv, mem_qn_g, mem_kn_g, w_mem_out, w_o):
    batch, seq, _ = x.shape
    half = QK_ROPE // 2
    inv_freq = jnp.power(ROPE_THETA, -jnp.arange(half, dtype=F32) / half)
    ang = positions.astype(F32)[..., None] * inv_freq
    cost = jnp.cos(ang).reshape(batch * seq, half).T
    sint = jnp.sin(ang).reshape(batch * seq, half).T
    for l in range(norm_g.shape[0]):
        x = _layer(x, cost, sint, mem, norm_g[l], w_in[l], conv_w[l], w_conv_out[l], mla_q_norm_g[l],
                   w_uq[l], mla_kv_norm_g[l], w_ukv[l], mla_qn_nope_g[l], mla_qn_rope_g[l],
                   mla_kn_nope_g[l], mla_kn_rope_g[l], w_mla_out[l], mem_norm_g[l], w_mem_kv[l],
                   mem_qn_g[l], mem_kn_g[l], w_mem_out[l], w_o[l])
    return x
```

```python
import functools

import jax
import jax.numpy as jnp
from jax import lax
from jax.experimental import pallas as pl
from jax.experimental.pallas import tpu as pltpu

F32 = jnp.float32
BF16 = jnp.bfloat16

EPS = 1e-6
CHUNK = 64
CONV_WIDTH = 1024
CONV_K = 3
CONV_GROUP = 256
MLA_HEADS = 16
Q_LORA = 512
KV_LORA = 512
QK_NOPE = 128
QK_ROPE = 64
V_HEAD = 128
MLA_WIDTH = MLA_HEADS * V_HEAD
HEAD_PAD = 256
ROPE_THETA = 10000.0
MEM_HEADS = 4
MEM_HEAD_DIM = 256
MEM_WIDTH = MEM_HEADS * MEM_HEAD_DIM
NEG_BIG = -1e30
LOG2_E = 1.4426950408889634
ATTN_TILE = 512
ATTN_GROUP = 4
ATTN_HEADS_PER_STEP = 2

PROJ_ROW_TILE = 1024
GATE_COL_TILE = 2048
MEM_ROW_TILE = 1024
PREP_ROW_TILE = 256
MERGE_ROW_TILE = 256

SUBLANES = 8
BF16_SUBLANES = 16
VMEM_LIMIT_BYTES = 52 * 1024 * 1024
MERGE_VMEM_LIMIT_BYTES = 50 * 1024 * 1024


def _params(*sem):
    return pltpu.CompilerParams(dimension_semantics=sem, vmem_limit_bytes=VMEM_LIMIT_BYTES)


def _rms(x, width):
    ms = jnp.sum(x * x, axis=-1, keepdims=True) * (1.0 / width)
    return x * lax.rsqrt(ms + EPS)


def _sigmoid(x):
    return 0.5 * jnp.tanh(0.5 * x) + 0.5


def _dot(a, b):
    return jnp.dot(a, b, preferred_element_type=F32)


def _dot_nt(a, b):
    return lax.dot_general(a, b, (((1,), (1,)), ((), ())), preferred_element_type=F32)


def _conv_kernel(x_ref, ng_ref, wc_ref, wb_ref, wu_ref, wz_ref, cw_ref, wf_ref, o_ref, h_ref, wr_ref,
                 carry_ref, buf_ref, *, tm, tiles_per_seq):
    i = pl.program_id(0)
    g = pl.program_id(1)
    wr_ref[...] = wf_ref[...].astype(BF16)

    @pl.when(g == 0)
    def _():
        x = x_ref[...]
        h_ref[...] = (_rms(x, x.shape[-1]) * ng_ref[...]).astype(BF16)

    h = h_ref[...]
    b = _dot_nt(h, wb_ref[...])
    z = _dot_nt(h, wz_ref[...])
    cu = _dot_nt(h, wc_ref[...]) * _dot_nt(h, wu_ref[...])
    seq_start = (i % tiles_per_seq) == 0
    buf_ref[0:SUBLANES, :] = jnp.where(seq_start, 0.0, carry_ref[g])
    buf_ref[SUBLANES:SUBLANES + tm, :] = cu
    carry_ref[g] = cu[tm - SUBLANES:tm, :]
    cw = cw_ref[...]
    y = cu * cw[CONV_K - 1:CONV_K, :]
    for tap in range(CONV_K - 1):
        lag = CONV_K - 1 - tap
        y = y + buf_ref[SUBLANES - lag:SUBLANES - lag + tm, :] * cw[tap:tap + 1, :]
    o_ref[...] = (b * y * (z * _sigmoid(z))).astype(o_ref.dtype)


def _norm_conv_branch(x2d, norm_g, w_front, conv_w, w_in_t, row0, seq, tm):
    t, d = x2d.shape
    n_groups = CONV_WIDTH // CONV_GROUP
    n_steps = (t // tm) * n_groups
    rest = w_in_t.shape[0] - row0
    slab = rest // n_steps
    assert slab * n_steps == rest and slab % BF16_SUBLANES == 0 and row0 % SUBLANES == 0
    kern = functools.partial(_conv_kernel, tm=tm, tiles_per_seq=seq // tm)
    part = lambda k: pl.BlockSpec((CONV_GROUP, d), lambda i, g: (k * n_groups + g, 0))
    return pl.pallas_call(
        kern,
        out_shape=(jax.ShapeDtypeStruct((t, CONV_WIDTH), BF16), jax.ShapeDtypeStruct((t, d), BF16),
                   jax.ShapeDtypeStruct((rest, d), BF16)),
        grid=(t // tm, n_groups),
        in_specs=[pl.BlockSpec((tm, d), lambda i, g: (i, 0)),
                  pl.BlockSpec((1, d), lambda i, g: (0, 0)),
                  part(0), part(1), part(2), part(3),
                  pl.BlockSpec((CONV_K, CONV_GROUP), lambda i, g: (0, g)),
                  pl.BlockSpec((pl.Element(slab), pl.Element(d)),
                               lambda i, g: (pl.multiple_of(row0 + (i * n_groups + g) * slab, SUBLANES), 0))],
        out_specs=(pl.BlockSpec((tm, CONV_GROUP), lambda i, g: (i, g)),
                   pl.BlockSpec((tm, d), lambda i, g: (i, 0)),
                   pl.BlockSpec((slab, d), lambda i, g: (i * n_groups + g, 0))),
        scratch_shapes=[pltpu.VMEM((n_groups, SUBLANES, CONV_GROUP), F32),
                        pltpu.VMEM((tm + SUBLANES, CONV_GROUP), F32)],
        compiler_params=_params("arbitrary", "arbitrary"),
        name="norm_conv_branch",
    )(x2d, norm_g, w_front, w_front, w_front, w_front, conv_w, w_in_t)


def _rope_t(x, g, cost, sint):
    half = QK_ROPE // 2
    x = x * lax.rsqrt(jnp.sum(x * x, axis=0, keepdims=True) * (1.0 / QK_ROPE) + EPS) * g
    x1 = x[0:half, :]
    x2 = x[half:QK_ROPE, :]
    return x1 * cost - x2 * sint, x2 * cost + x1 * sint


def _mla_prep_kernel(h_ref, win_ref, wkrt_ref, wuqt_ref, wkn_ref, wvt_ref, gq_ref, gkv_ref, gqn_ref,
                     gqr_ref, gkn_ref, gkr_ref, cost_ref, sint_ref, qt_ref, k_ref, vt_ref):
    tm = h_ref.shape[0]
    h = h_ref[...]
    p = _dot_nt(h, win_ref[...])
    cqn = (_rms(p[:, 0:Q_LORA], Q_LORA) * gq_ref[...]).astype(BF16)
    ckvn = (_rms(p[:, Q_LORA:Q_LORA + KV_LORA], KV_LORA) * gkv_ref[...]).astype(BF16)
    qt = _dot_nt(wuqt_ref[...], cqn)
    kn = _dot(ckvn, wkn_ref[...])
    vt = _dot_nt(wvt_ref[...], ckvn)
    krt = _dot_nt(wkrt_ref[...], h)

    cost = cost_ref[...]
    sint = sint_ref[...]
    kr1, kr2 = _rope_t(krt, jnp.broadcast_to(gkr_ref[...], (QK_ROPE, tm)), cost, sint)
    zeros_f = jnp.zeros((HEAD_PAD - QK_NOPE - QK_ROPE, tm), F32)
    k_rope = jnp.concatenate([kr1, kr2, zeros_f], axis=0).T.astype(BF16)

    gqn = jnp.broadcast_to(gqn_ref[...], (QK_NOPE, tm))
    gqr = jnp.broadcast_to(gqr_ref[...], (QK_ROPE, tm))
    gkn = gkn_ref[...]
    zeros = zeros_f.astype(BF16)
    for hd in range(MLA_HEADS):
        lo = hd * (QK_NOPE + QK_ROPE)
        qn = qt[lo:lo + QK_NOPE, :]
        qr = qt[lo + QK_NOPE:lo + QK_NOPE + QK_ROPE, :]
        qn = qn * lax.rsqrt(jnp.sum(qn * qn, axis=0, keepdims=True) * (1.0 / QK_NOPE) + EPS) * gqn
        r1, r2 = _rope_t(qr, gqr, cost, sint)
        half = QK_ROPE // 2
        qt_ref[0, hd, 0:QK_NOPE, :] = qn.astype(BF16)
        qt_ref[0, hd, QK_NOPE:QK_NOPE + half, :] = r1.astype(BF16)
        qt_ref[0, hd, QK_NOPE + half:QK_NOPE + QK_ROPE, :] = r2.astype(BF16)
        qt_ref[0, hd, QK_NOPE + QK_ROPE:HEAD_PAD, :] = zeros
        kh = kn[:, hd * QK_NOPE:(hd + 1) * QK_NOPE]
        k_ref[0, hd, :, 0:QK_NOPE] = (_rms(kh, QK_NOPE) * gkn).astype(BF16)
        k_ref[0, hd, :, QK_NOPE:HEAD_PAD] = k_rope
        vt_ref[0, hd, 0, :, :] = vt[hd * V_HEAD:(hd + 1) * V_HEAD, :].astype(BF16)


def _mla_prep(h, w_front, w_uqt, w_kn, w_vt, gq, gkv, gqn, gqr, gkn, gkr, cost, sint,
              batch, seq, tm, tk):
    t, d = h.shape
    ns = seq // tm
    per_chunk = tk // tm
    lora = Q_LORA + KV_LORA
    const = lambda b, s: (0, 0)
    row = lambda b, s: (b * ns + s, 0)
    col = lambda b, s: (0, b * ns + s)
    qt_shape = jax.ShapeDtypeStruct((batch, MLA_HEADS, HEAD_PAD, seq), BF16)
    k_shape = jax.ShapeDtypeStruct((batch, MLA_HEADS, seq, HEAD_PAD), BF16)
    vt_shape = jax.ShapeDtypeStruct((batch, MLA_HEADS, seq // tk, V_HEAD, tk), BF16)
    full = lambda a: pl.BlockSpec(a.shape, const)
    return pl.pallas_call(
        _mla_prep_kernel,
        out_shape=(qt_shape, k_shape, vt_shape),
        grid=(batch, ns),
        in_specs=[pl.BlockSpec((tm, d), row),
                  pl.BlockSpec((lora, d), lambda b, s: (4 * CONV_WIDTH // lora, 0)),
                  pl.BlockSpec((QK_ROPE, d), lambda b, s: ((4 * CONV_WIDTH + lora) // QK_ROPE, 0)),
                  full(w_uqt), full(w_kn), full(w_vt),
                  full(gq), full(gkv), full(gqn), full(gqr), full(gkn), full(gkr),
                  pl.BlockSpec((QK_ROPE // 2, tm), col), pl.BlockSpec((QK_ROPE // 2, tm), col)],
        out_specs=(pl.BlockSpec((1, MLA_HEADS, HEAD_PAD, tm), lambda b, s: (b, 0, 0, s)),
                   pl.BlockSpec((1, MLA_HEADS, tm, HEAD_PAD), lambda b, s: (b, 0, s, 0)),
                   pl.BlockSpec((1, MLA_HEADS, 1, V_HEAD, tm),
                                lambda b, s: (b, 0, s // per_chunk, 0, s % per_chunk))),
        compiler_params=_params("arbitrary", "arbitrary"),
        name="mla_prep",
    )(h, w_front, w_front, w_uqt, w_kn, w_vt, gq, gkv, gqn, gqr, gkn, gkr, cost, sint)


def _attn_kernel(qt_ref, qtn_ref, k_ref, vt_ref, sz_ref, o_ref, acc_ref, m_ref, s0_ref, s1_ref,
                 *, tq, heads):
    j = pl.program_id(2)
    s_bufs = (s0_ref, s1_ref)

    def scores(hh, ki, q, dst):
        start = pl.multiple_of(ki * tq, tq)
        s = _dot(k_ref[0, hh, pl.ds(start, tq), :], q)
        dst[hh, 0:tq, :] = s
        dst[hh, tq:tq + 1, :] = jnp.max(s, axis=0, keepdims=True)

    def update(hh, ki, src, diagonal):
        s = src[hh, 0:tq, :]
        if diagonal:
            kc = lax.broadcasted_iota(jnp.int32, s.shape, 0) // CHUNK
            qc = lax.broadcasted_iota(jnp.int32, s.shape, 1) // CHUNK
            s = jnp.where(kc <= qc, s, NEG_BIG)
            m_tile = jnp.max(s, axis=0, keepdims=True)
        else:
            m_tile = src[hh, tq:tq + 1, :]
        m = m_ref[hh]
        m_new = jnp.maximum(m, m_tile)
        alpha = jnp.exp2(m - m_new)
        p = jnp.exp2(s - m_new)
        m_ref[hh] = m_new
        vt1 = jnp.concatenate([vt_ref[0, hh, ki], ones_rows], axis=0)
        acc_ref[hh] = alpha * acc_ref[hh] + _dot(vt1, p.astype(BF16))

    ones_rows = (lax.broadcasted_iota(jnp.int32, (BF16_SUBLANES, tq), 0) == 0).astype(BF16)

    @pl.when(j == 0)
    def _():
        for hh in range(heads):
            scores(hh, 0, qt_ref[0, hh, :, 0:tq], s0_ref)

    for u in range(ATTN_GROUP):
        q_tile = ATTN_GROUP * j + u
        first = ((u + 1) // 2) % 2
        bufs = (s_bufs[first], s_bufs[1 - first])

        def q_cur(hh, u=u):
            return qt_ref[0, hh, :, u * tq:(u + 1) * tq]

        def q_next(hh, u=u):
            if u + 1 < ATTN_GROUP:
                return qt_ref[0, hh, :, (u + 1) * tq:(u + 2) * tq]
            return qtn_ref[0, hh]

        acc_ref[...] = jnp.zeros_like(acc_ref)
        m_ref[...] = jnp.full(m_ref.shape, NEG_BIG, F32)

        def step(ki, parity, bufs=bufs, q_cur=q_cur):
            for hh in range(heads):
                scores(hh, ki + 1, q_cur(hh), bufs[1 - parity])
                update(hh, ki, bufs[parity], False)

        def group(jj, carry, step=step):
            for r in range(ATTN_GROUP):
                step(ATTN_GROUP * jj + r, r % 2)
            return carry

        lax.fori_loop(0, j, group, 0)
        for r in range(u):
            step(ATTN_GROUP * j + r, r % 2)
        for hh in range(heads):
            scores(hh, 0, q_next(hh), bufs[1 - u % 2])
            update(hh, q_tile, bufs[u % 2], True)
            l = acc_ref[hh, V_HEAD:V_HEAD + 1, :]
            y = (acc_ref[hh, 0:V_HEAD, :] * (1.0 / l)).T
            sz = sz_ref[0, u * tq:(u + 1) * tq, hh * V_HEAD:(hh + 1) * V_HEAD].astype(F32)
            o_ref[0, u * tq:(u + 1) * tq, hh * V_HEAD:(hh + 1) * V_HEAD] = (y * sz).astype(o_ref.dtype)


def _mla_attention(qt, k, vt, sz, tq, heads):
    batch, n_heads, seq, _ = k.shape
    assert ATTN_GROUP % 4 == 0
    kern = functools.partial(_attn_kernel, tq=tq, heads=heads)
    tg = ATTN_GROUP * tq
    last = seq // tq - 1
    return pl.pallas_call(
        kern,
        out_shape=jax.ShapeDtypeStruct((batch, seq, MLA_WIDTH), BF16),
        grid=(batch, n_heads // heads, seq // tg),
        in_specs=[pl.BlockSpec((1, heads, HEAD_PAD, tg), lambda b, h, i: (b, h, 0, i)),
                  pl.BlockSpec((1, heads, HEAD_PAD, tq),
                               lambda b, h, i: (b, h, 0, jnp.minimum(ATTN_GROUP * (i + 1), last))),
                  pl.BlockSpec((1, heads, seq, HEAD_PAD), lambda b, h, i: (b, h, 0, 0)),
                  pl.BlockSpec((1, heads, seq // tq, V_HEAD, tq), lambda b, h, i: (b, h, 0, 0, 0)),
                  pl.BlockSpec((1, tg, heads * V_HEAD), lambda b, h, i: (b, i, h))],
        out_specs=pl.BlockSpec((1, tg, heads * V_HEAD), lambda b, h, i: (b, i, h)),
        scratch_shapes=[pltpu.VMEM((heads, V_HEAD + BF16_SUBLANES, tq), F32),
                        pltpu.VMEM((heads, 1, tq), F32),
                        pltpu.VMEM((heads, tq + SUBLANES, tq), F32),
                        pltpu.VMEM((heads, tq + SUBLANES, tq), F32)],
        compiler_params=_params("arbitrary", "arbitrary", "arbitrary"),
        name="mla_attention",
    )(qt, qt, k, vt, sz)


def _mem_kv_kernel(mem_ref, g_ref, w_ref, gk_ref, k_ref, v_ref):
    m = mem_ref[0]
    kv = _dot((_rms(m, m.shape[-1]) * g_ref[...]).astype(BF16), w_ref[...].astype(BF16))
    gk = gk_ref[...]
    for hd in range(MEM_HEADS):
        lo = hd * MEM_HEAD_DIM
        k_ref[0, :, lo:lo + MEM_HEAD_DIM] = (_rms(kv[:, lo:lo + MEM_HEAD_DIM], MEM_HEAD_DIM) * gk).astype(BF16)
    v_ref[0] = kv[:, MEM_WIDTH:].astype(BF16)


def _mem_kv(mem, g, w, gk):
    batch, m, d = mem.shape
    const = lambda b: (0, 0)
    shape = jax.ShapeDtypeStruct((batch, m, MEM_WIDTH), BF16)
    return pl.pallas_call(
        _mem_kv_kernel,
        out_shape=(shape, shape),
        grid=(batch,),
        in_specs=[pl.BlockSpec((1, m, d), lambda b: (b, 0, 0)),
                  pl.BlockSpec(g.shape, const),
                  pl.BlockSpec(w.shape, const, pipeline_mode=pl.Buffered(1)),
                  pl.BlockSpec(gk.shape, const)],
        out_specs=(pl.BlockSpec((1, m, MEM_WIDTH), lambda b: (b, 0, 0)),
                   pl.BlockSpec((1, m, MEM_WIDTH), lambda b: (b, 0, 0))),
        compiler_params=_params("arbitrary"),
        name="mem_kv",
    )(mem, g, w, gk)


def _mem_attn_kernel(h_ref, w_ref, k_ref, v_ref, gq_ref, o_ref):
    p = _dot_nt(h_ref[...], w_ref[...])
    gq = gq_ref[...] * (MEM_HEAD_DIM ** -0.5)
    for hd in range(MEM_HEADS):
        lo = hd * MEM_HEAD_DIM
        q = (_rms(p[:, lo:lo + MEM_HEAD_DIM], MEM_HEAD_DIM) * gq).astype(BF16)
        z = p[:, MEM_WIDTH + lo:MEM_WIDTH + lo + MEM_HEAD_DIM]
        s = _dot_nt(q, k_ref[0, :, lo:lo + MEM_HEAD_DIM])
        e = jnp.exp(s - jnp.max(s, axis=-1, keepdims=True))
        y = _dot(e.astype(BF16), v_ref[0, :, lo:lo + MEM_HEAD_DIM])
        y = y * (1.0 / jnp.sum(e, axis=-1, keepdims=True))
        o_ref[:, lo:lo + MEM_HEAD_DIM] = (y * (z * _sigmoid(z))).astype(o_ref.dtype)


def _mem_attention(h, w, col0, k_mem, v_mem, gq, batch, seq, tm):
    t, d = h.shape
    ns = seq // tm
    m = k_mem.shape[1]
    const = lambda b, s: (0, 0)
    return pl.pallas_call(
        _mem_attn_kernel,
        out_shape=jax.ShapeDtypeStruct((t, MEM_WIDTH), BF16),
        grid=(batch, ns),
        in_specs=[pl.BlockSpec((tm, d), lambda b, s: (b * ns + s, 0)),
                  pl.BlockSpec((pl.Element(2 * MEM_WIDTH), pl.Element(d)), lambda b, s: (col0, 0)),
                  pl.BlockSpec((1, m, MEM_WIDTH), lambda b, s: (b, 0, 0)),
                  pl.BlockSpec((1, m, MEM_WIDTH), lambda b, s: (b, 0, 0)),
                  pl.BlockSpec(gq.shape, const)],
        out_specs=pl.BlockSpec((tm, MEM_WIDTH), lambda b, s: (b * ns + s, 0)),
        compiler_params=_params("arbitrary", "arbitrary"),
        name="mem_attention",
    )(h, w, k_mem, v_mem, gq)


def _gate_kernel(h_ref, w_ref, *refs, n_silu):
    n_cast = (len(refs) - 1) // 2
    o_ref = refs[n_cast]
    for src, dst in zip(refs[:n_cast], refs[n_cast + 1:]):
        dst[...] = src[...].astype(BF16)
    j = pl.program_id(0)

    @pl.when(j < n_silu)
    def _():
        p = _dot_nt(h_ref[...], w_ref[...])
        o_ref[...] = (p * _sigmoid(p)).astype(o_ref.dtype)

    @pl.when(j >= n_silu)
    def _():
        o_ref[...] = _sigmoid(_dot_nt(h_ref[...], w_ref[...])).astype(o_ref.dtype)


def _gate_proj(h, w, n, n_silu, skip_rows, tm, tn, cast=()):
    t, d = h.shape
    col0 = 0
    assert skip_rows % BF16_SUBLANES == 0 and n % tn == 0
    ni, nj = t // tm, n // tn
    n_steps = ni * nj
    slabs = []
    for a in cast:
        assert a.shape[0] % (n_steps * BF16_SUBLANES) == 0
        slabs.append(pl.BlockSpec((a.shape[0] // n_steps, a.shape[1]), lambda j, i: (j * ni + i, 0)))
    out = pl.pallas_call(
        functools.partial(_gate_kernel, n_silu=n_silu),
        out_shape=(jax.ShapeDtypeStruct((t, n), BF16), *[jax.ShapeDtypeStruct(a.shape, BF16) for a in cast]),
        grid=(nj, ni),
        in_specs=[pl.BlockSpec((tm, d), lambda j, i: (i, 0)),
                  pl.BlockSpec((pl.Element(tn), pl.Element(d)),
                               lambda j, i: (pl.multiple_of(
                                   col0 + j * tn + jnp.where(j >= n_silu, skip_rows, 0), BF16_SUBLANES), 0)),
                  *slabs],
        out_specs=(pl.BlockSpec((tm, tn), lambda j, i: (i, j)), *slabs),
        compiler_params=_params("arbitrary", "arbitrary"),
        name="gate_proj",
    )(h, w, *cast)
    return out if cast else out[0]


def _merge_out_kernel(ac_ref, am_ref, ae_ref, gc_ref, gm_ref, ge_ref, x_ref, wc_ref, wm_ref, we_ref,
                      wo_ref, o_ref):
    merged = (gc_ref[...].astype(F32) * _dot(ac_ref[...], wc_ref[...])
              + gm_ref[...].astype(F32) * _dot(am_ref[...], wm_ref[...])
              + ge_ref[...].astype(F32) * _dot(ae_ref[...], we_ref[...]))
    o_ref[...] = x_ref[...] + _dot(merged.astype(BF16), wo_ref[...])


def _merge_out(a_conv, a_mla, a_mem, gates, gate0, x2d, w_conv, w_mla, w_mem, w_o, tm):
    t, d = x2d.shape
    row = lambda a: pl.BlockSpec((tm, a.shape[1]), lambda i: (i, 0))
    gate = lambda k: pl.BlockSpec((tm, d), lambda i: (i, gate0 + k))
    wgt = lambda w: pl.BlockSpec(w.shape, lambda i: (0, 0), pipeline_mode=pl.Buffered(1))
    return pl.pallas_call(
        _merge_out_kernel,
        out_shape=jax.ShapeDtypeStruct((t, d), F32),
        grid=(t // tm,),
        in_specs=[row(a_conv), row(a_mla), row(a_mem), gate(0), gate(1), gate(2), row(x2d),
                  wgt(w_conv), wgt(w_mla), wgt(w_mem), wgt(w_o)],
        out_specs=pl.BlockSpec((tm, d), lambda i: (i, 0)),
        compiler_params=pltpu.CompilerParams(dimension_semantics=("arbitrary",),
                                             vmem_limit_bytes=MERGE_VMEM_LIMIT_BYTES),
        name="merge_out",
    )(a_conv, a_mla, a_mem, gates, gates, gates, x2d, w_conv, w_mla, w_mem, w_o)


def _layer(x, cost, sint, mem, norm_g, w_in, conv_w, w_conv_out, mla_q_norm_g, w_uq, mla_kv_norm_g,
           w_ukv, mla_qn_nope_g, mla_qn_rope_g, mla_kn_nope_g, mla_kn_rope_g, w_mla_out, mem_norm_g,
           w_mem_kv, mem_qn_g, mem_kn_g, w_mem_out, w_o):
    batch, seq, d = x.shape
    t = batch * seq
    x2d = x.reshape(t, d)

    o_rest = 4 * CONV_WIDTH + Q_LORA + KV_LORA + QK_ROPE
    w_in_t = w_in.T
    w_front = w_in_t[:o_rest].astype(BF16)
    w_uqt = w_uq.astype(BF16).T
    w_ukv_h = w_ukv.astype(BF16).reshape(KV_LORA, MLA_HEADS, QK_NOPE + V_HEAD)
    w_kn = w_ukv_h[:, :, :QK_NOPE].reshape(KV_LORA, MLA_HEADS * QK_NOPE)
    w_vt = w_ukv_h[:, :, QK_NOPE:].reshape(KV_LORA, MLA_WIDTH).T
    qscale = (QK_NOPE + QK_ROPE) ** -0.5 * LOG2_E

    conv_act, h, w_rest = _norm_conv_branch(x2d, norm_g[None, :], w_front, conv_w, w_in_t, o_rest, seq,
                                            tm=PROJ_ROW_TILE)
    qt, k, vt = _mla_prep(h, w_front, w_uqt, w_kn, w_vt, mla_q_norm_g[None, :],
                          mla_kv_norm_g[None, :], (mla_qn_nope_g * qscale)[:, None],
                          (mla_qn_rope_g * qscale)[:, None], mla_kn_nope_g[None, :],
                          mla_kn_rope_g[:, None], cost, sint, batch, seq, tm=PREP_ROW_TILE, tk=ATTN_TILE)
    assert MLA_WIDTH == GATE_COL_TILE == d
    gates, wb_conv, wb_mla, wb_mem, wb_o = _gate_proj(
        h, w_rest, MLA_WIDTH + 3 * d, 1, 2 * MEM_WIDTH, tm=PROJ_ROW_TILE, tn=GATE_COL_TILE,
        cast=(w_conv_out, w_mla_out, w_mem_out, w_o))
    mla_act = _mla_attention(qt, k, vt, gates.reshape(batch, seq, MLA_WIDTH + 3 * d), tq=ATTN_TILE,
                             heads=ATTN_HEADS_PER_STEP)
    k_mem, v_mem = _mem_kv(mem, mem_norm_g[None, :], w_mem_kv, mem_kn_g[None, :])
    mem_act = _mem_attention(h, w_rest, MLA_WIDTH, k_mem, v_mem, mem_qn_g[None, :], batch, seq,
                             tm=MEM_ROW_TILE)
    out = _merge_out(conv_act, mla_act.reshape(t, MLA_WIDTH), mem_act, gates, 1, x2d, wb_conv, wb_mla,
                     wb_mem, wb_o, tm=MERGE_ROW_TILE)
    return out.reshape(batch, seq, d)


def kernel(x, positions, mem, norm_g, w_in, conv_w, w_conv_out, mla_q_norm_g, w_uq, mla_kv_norm_g, w_ukv, mla_qn_nope_g, mla_qn_rope_g, mla_kn_nope_g, mla_kn_rope_g, w_mla_out, mem_norm_g, w_mem_kv, mem_qn_g, mem_kn_g, w_mem_out, w_o):
    batch, seq, _ = x.shape
    half = QK_ROPE // 2
    inv_freq = jnp.power(ROPE_THETA, -jnp.arange(half, dtype=F32) / half)
    ang = positions.astype(F32)[..., None] * inv_freq
    cost = jnp.cos(ang).reshape(batch * seq, half).T
    sint = jnp.sin(ang).reshape(batch * seq, half).T
    for l in range(norm_g.shape[0]):
        x = _layer(x, cost, sint, mem, norm_g[l], w_in[l], conv_w[l], w_conv_out[l], mla_q_norm_g[l],
                   w_uq[l], mla_kv_norm_g[l], w_ukv[l], mla_qn_nope_g[l], mla_qn_rope_g[l],
                   mla_kn_nope_g[l], mla_kn_rope_g[l], w_mla_out[l], mem_norm_g[l], w_mem_kv[l],
                   mem_qn_g[l], mem_kn_g[l], w_mem_out[l], w_o[l])
    return x
```
